```python
import math
import jax
import jax.numpy as jnp
from jax import lax
import numpy as np

D_MODEL = 2048
BATCH = 8
SEQ = 2048
DEPTH = 2
DEC_BATCH = 32
DEC_SEQ = 1
PAST_LEN = 8192
PAGE_SIZE = 128

N_MIXERS = 2
N_MLSTM_LAYERS = (DEPTH + 1) // 2
N_SB_LAYERS = DEPTH // 2
ML_HEADS = 4
ML_DV = D_MODEL // ML_HEADS
ML_DK = ML_DV // 2
ML_CHUNK = 64
GATE_CAP = 15.0
SB_HEADS = 16
SB_HEAD_DIM = D_MODEL // SB_HEADS
SB_BLOCK = 128
SB_BIAS_INIT = -7.0
D_FF = 4 * D_MODEL
EPS = 1e-6
ML_SPLITS = (ML_HEADS * ML_DK, 2 * ML_HEADS * ML_DK,
             2 * ML_HEADS * ML_DK + ML_HEADS * ML_DV,
             2 * ML_HEADS * ML_DK + 2 * ML_HEADS * ML_DV)
ML_IN = 2 * ML_HEADS * ML_DK + 2 * ML_HEADS * ML_DV + 2 * ML_HEADS
SB_IN = 3 * SB_HEADS * SB_HEAD_DIM

kernel_name = 'hybrid_mlstm_stickbreaking_step'


def rmsnorm(x, g):
    xf = x.astype(jnp.float32)
    y = xf * lax.rsqrt(jnp.mean(xf * xf, axis=-1, keepdims=True) + EPS)
    return (y * g.astype(jnp.float32)).astype(x.dtype)


def soft_cap(x):
    return GATE_CAP * jnp.tanh(x / GATE_CAP)


def to_chunks(a, n_chunks, chunk):
    a = a.reshape(a.shape[:2] + (n_chunks, chunk) + a.shape[3:])
    return jnp.moveaxis(a, 2, 0)


def mlstm_chunkwise(q, k, v, log_i, log_f, c0, n0, m0, chunk):
    b_sz, n_h, seq_len, _ = q.shape
    n_chunks = seq_len // chunk
    causal = jnp.tril(jnp.ones((chunk, chunk), dtype=bool))
    xs = tuple(to_chunks(a, n_chunks, chunk) for a in (q, k, v, log_i, log_f))

    def step(carry, inp):
        c_st, n_st, m_st = carry
        qc, kc, vc, ic, fc = inp
        b = jnp.cumsum(fc, axis=-1)
        log_d = jnp.where(causal, b[..., :, None] - b[..., None, :] + ic[..., None, :], -jnp.inf)
        log_inter = b + m_st[..., None]
        m_row = jnp.maximum(log_inter, jnp.max(log_d, axis=-1))
        s = jnp.einsum('bhtk,bhsk->bhts', qc, kc) * jnp.exp(log_d - m_row[..., None])
        w_inter = jnp.exp(log_inter - m_row)
        num = (w_inter[..., None] * jnp.einsum('bhtk,bhkv->bhtv', qc, c_st)
               + jnp.einsum('bhts,bhsv->bhtv', s, vc))
        den = w_inter * jnp.einsum('bhtk,bhk->bht', qc, n_st) + jnp.sum(s, axis=-1)
        h = num / jnp.maximum(jnp.abs(den), jnp.exp(-m_row))[..., None]
        b_last = b[..., -1]
        log_w = b_last[..., None] - b + ic
        m_new = jnp.maximum(b_last + m_st, jnp.max(log_w, axis=-1))
        w = jnp.exp(log_w - m_new[..., None])
        decay = jnp.exp(b_last + m_st - m_new)
        c_new = decay[..., None, None] * c_st + jnp.einsum('bhs,bhsk,bhsv->bhkv', w, kc, vc)
        n_new = decay[..., None] * n_st + jnp.einsum('bhs,bhsk->bhk', w, kc)
        return (c_new, n_new, m_new), h

    (c_f, n_f, m_f), hs = lax.scan(step, (c0, n0, m0), xs)
    hs = jnp.moveaxis(hs, 0, 2).reshape(b_sz, n_h, seq_len, hs.shape[-1])
    return hs, c_f, n_f, m_f


def mlstm_mixer(h, c0, n0, m0, w_in, b_gate, g_head, w_out):
    b_sz, seq_len, _ = h.shape
    q, k, v, o, gates = jnp.split(h @ w_in, ML_SPLITS, axis=-1)

    def heads(a, d):
        return a.reshape(b_sz, seq_len, ML_HEADS, d).transpose(0, 2, 1, 3).astype(jnp.float32)

    q = heads(q, ML_DK)
    k = heads(k, ML_DK) * (ML_DK ** -0.5)
    v = heads(v, ML_DV)
    gates = soft_cap(gates.astype(jnp.float32) + b_gate.astype(jnp.float32))
    log_i = gates[..., :ML_HEADS].transpose(0, 2, 1)
    log_f = jax.nn.log_sigmoid(gates[..., ML_HEADS:]).transpose(0, 2, 1)
    chunk = math.gcd(seq_len, ML_CHUNK)
    hs, c_f, n_f, m_f = mlstm_chunkwise(q, k, v, log_i, log_f, c0.astype(jnp.float32),
                                        n0.astype(jnp.float32), m0.astype(jnp.float32), chunk)
    hs = hs * lax.rsqrt(jnp.mean(hs * hs, axis=-1, keepdims=True) + EPS)
    hs = hs.transpose(0, 2, 1, 3).reshape(b_sz, seq_len, ML_HEADS * ML_DV)
    hs = hs * g_head.astype(jnp.float32) * jax.nn.sigmoid(o.astype(jnp.float32))
    return hs.astype(h.dtype) @ w_out, c_f, n_f, m_f


def stick_breaking_weights(z, mask):
    log_om = jnp.where(mask, jax.nn.log_sigmoid(-z), 0.0)
    between = lax.cumsum(log_om, axis=3, reverse=True) - log_om
    return jnp.where(mask, jnp.exp(jax.nn.log_sigmoid(z) + between), 0.0)


def sb_heads(h, w_in):
    b_sz, seq_len, _ = h.shape
    q, k, v = jnp.split(h @ w_in, 3, axis=-1)
    shape = (b_sz, seq_len, SB_HEADS, SB_HEAD_DIM)
    return q.reshape(shape), k.reshape(shape), v.reshape(shape)


def sb_prompt(h, w_in, logit_bias, w_out):
    q, k, v = sb_heads(h, w_in)
    b_sz, seq_len = h.shape[:2]
    scale = SB_HEAD_DIM ** -0.5
    bias = logit_bias.astype(jnp.float32)[None, :, None, None]
    outs = []
    for start in range(0, seq_len, SB_BLOCK):
        end = min(start + SB_BLOCK, seq_len)
        z = jnp.einsum('bqhd,bkhd->bhqk', q[:, start:end], k[:, :end]).astype(jnp.float32) * scale + bias
        mask = jnp.arange(end)[None, :] < jnp.arange(start, end)[:, None]
        a = stick_breaking_weights(z, mask).astype(v.dtype)
        outs.append(jnp.einsum('bhqk,bkhd->bqhd', a, v[:, :end]))
    o = jnp.concatenate(outs, axis=1).reshape(b_sz, seq_len, D_MODEL)
    return o @ w_out, k, v


def sb_sample(h, k_pool, v_pool, page_table, w_in, logit_bias, w_out):
    q, k, v = sb_heads(h, w_in)
    b_sz, seq_len = h.shape[:2]
    past_len = page_table.shape[1] * k_pool.shape[1]
    k_past = k_pool[page_table].reshape(b_sz, past_len, SB_HEADS, SB_HEAD_DIM)
    v_past = v_pool[page_table].reshape(b_sz, past_len, SB_HEADS, SB_HEAD_DIM)
    scale = SB_HEAD_DIM ** -0.5
    bias = logit_bias.astype(jnp.float32)[None, :, None, None]
    outs = []
    for start in range(0, seq_len, SB_BLOCK):
        end = min(start + SB_BLOCK, seq_len)
        qb = q[:, start:end]
        z_past = jnp.einsum('bqhd,bkhd->bhqk', qb, k_past).astype(jnp.float32)
        z_new = jnp.einsum('bqhd,bkhd->bhqk', qb, k[:, :end]).astype(jnp.float32)
        z = jnp.concatenate([z_past, z_new], axis=-1) * scale + bias
        k_pos = jnp.concatenate([jnp.arange(past_len), past_len + jnp.arange(end)])
        q_pos = past_len + jnp.arange(start, end)
        mask = k_pos[None, :] < q_pos[:, None]
        a = stick_breaking_weights(z, mask).astype(v.dtype)
        outs.append(jnp.einsum('bhqk,bkhd->bqhd', a[..., :past_len], v_past)
                    + jnp.einsum('bhqk,bkhd->bqhd', a[..., past_len:], v[:, :end]))
    o = jnp.concatenate(outs, axis=1).reshape(b_sz, seq_len, D_MODEL)
    return o @ w_out, k, v


def sq_relu_mlp(h, w_up, w_down):
    return jnp.square(jax.nn.relu(h @ w_up)) @ w_down


def setup_inputs(seed: int = 0) -> dict:
    key = jax.random.key(seed)
    ks = jax.random.split(key, 24)
    n_pages = PAST_LEN // PAGE_SIZE
    n_pool = (DEC_BATCH * n_pages * 5) // 4

    def nrm(k, shape):
        return jax.random.normal(k, shape, jnp.float32)

    def wt(k, shape, fan_in):
        return nrm(k, shape) * fan_in ** -0.5

    def gain(k, shape):
        return 1.0 + 0.02 * nrm(k, shape)

    page_table = jax.random.permutation(ks[7], n_pool)[:DEC_BATCH * n_pages]
    page_table = page_table.reshape(DEC_BATCH, n_pages).astype(jnp.int32)
    b_gate = jnp.concatenate([0.1 * nrm(ks[11], (N_MLSTM_LAYERS, ML_HEADS)),
                              3.0 + 0.5 * nrm(ks[12], (N_MLSTM_LAYERS, ML_HEADS))], axis=-1)
    return {
        'x_prompt': nrm(ks[0], (BATCH, SEQ, D_MODEL)),
        'x_sample': nrm(ks[1], (DEC_BATCH, DEC_SEQ, D_MODEL)),
        'state_C': 0.1 * nrm(ks[2], (N_MLSTM_LAYERS, DEC_BATCH, ML_HEADS, ML_DK, ML_DV)),
        'state_n': nrm(ks[3], (N_MLSTM_LAYERS, DEC_BATCH, ML_HEADS, ML_DK)),
        'state_m': nrm(ks[4], (N_MLSTM_LAYERS, DEC_BATCH, ML_HEADS)),
        'cache_k': nrm(ks[5], (N_SB_LAYERS, n_pool, PAGE_SIZE, SB_HEADS, SB_HEAD_DIM)),
        'cache_v': nrm(ks[6], (N_SB_LAYERS, n_pool, PAGE_SIZE, SB_HEADS, SB_HEAD_DIM)),
        'page_table': page_table,
        'norm_mix_g': gain(ks[8], (DEPTH, D_MODEL)),
        'norm_ffn_g': gain(ks[9], (DEPTH, D_MODEL)),
        'ml_w_in': wt(ks[10], (N_MLSTM_LAYERS, D_MODEL, ML_IN), D_MODEL),
        'ml_b_gate': b_gate,
        'ml_head_g': gain(ks[13], (N_MLSTM_LAYERS, ML_HEADS * ML_DV)),
        'ml_w_out': wt(ks[14], (N_MLSTM_LAYERS, ML_HEADS * ML_DV, D_MODEL), ML_HEADS * ML_DV),
        'sb_w_in': wt(ks[15], (N_SB_LAYERS, D_MODEL, SB_IN), D_MODEL),
        'sb_logit_bias': SB_BIAS_INIT + 0.3 * nrm(ks[20], (N_SB_LAYERS, SB_HEADS)),
        'sb_w_out': wt(ks[16], (N_SB_LAYERS, SB_HEADS * SB_HEAD_DIM, D_MODEL), SB_HEADS * SB_HEAD_DIM),
        'ffn_w_up': wt(ks[17], (DEPTH, D_MODEL, D_FF), D_MODEL),
        'ffn_w_down': wt(ks[18], (DEPTH, D_FF, D_MODEL), D_FF),
        'final_g': gain(ks[19], (D_MODEL,)),
    }


def reference(x_prompt, x_sample, state_C, state_n, state_m, cache_k, cache_v, page_table,
              norm_mix_g, norm_ffn_g, ml_w_in, ml_b_gate, ml_head_g, ml_w_out,
              sb_w_in, sb_logit_bias, sb_w_out, ffn_w_up, ffn_w_down, final_g):
    yp, ys = x_prompt, x_sample
    c_p, n_p, m_p, k_p, v_p = [], [], [], [], []
    c_s, n_s, m_s, k_s, v_s = [], [], [], [], []
    for i in range(DEPTH):
        j = i // N_MIXERS
        hp = rmsnorm(yp, norm_mix_g[i])
        hs = rmsnorm(ys, norm_mix_g[i])
        if i % N_MIXERS == 0:
            b_p = hp.shape[0]
            c0 = jnp.zeros((b_p, ML_HEADS, ML_DK, ML_DV), jnp.float32)
            n0 = jnp.zeros((b_p, ML_HEADS, ML_DK), jnp.float32)
            m0 = jnp.zeros((b_p, ML_HEADS), jnp.float32)
            mix_p, cf, nf, mf = mlstm_mixer(hp, c0, n0, m0, ml_w_in[j], ml_b_gate[j],
                                            ml_head_g[j], ml_w_out[j])
            c_p.append(cf); n_p.append(nf); m_p.append(mf)
            mix_s, cf, nf, mf = mlstm_mixer(hs, state_C[j], state_n[j], state_m[j], ml_w_in[j],
                                            ml_b_gate[j], ml_head_g[j], ml_w_out[j])
            c_s.append(cf); n_s.append(nf); m_s.append(mf)
        else:
            mix_p, kk, vv = sb_prompt(hp, sb_w_in[j], sb_logit_bias[j], sb_w_out[j])
            k_p.append(kk); v_p.append(vv)
            mix_s, kk, vv = sb_sample(hs, cache_k[j], cache_v[j], page_table, sb_w_in[j],
                                      sb_logit_bias[j], sb_w_out[j])
            k_s.append(kk); v_s.append(vv)
        yp = yp + mix_p
        ys = ys + mix_s
        yp = yp + sq_relu_mlp(rmsnorm(yp, norm_ffn_g[i]), ffn_w_up[i], ffn_w_down[i])
        ys = ys + sq_relu_mlp(rmsnorm(ys, norm_ffn_g[i]), ffn_w_up[i], ffn_w_down[i])
    y_prompt = rmsnorm(yp, final_g)
    y_sample = rmsnorm(ys, final_g)
    return (y_prompt, y_sample,
            jnp.stack(c_p), jnp.stack(n_p), jnp.stack(m_p), jnp.stack(k_p), jnp.stack(v_p),
            jnp.stack(c_s), jnp.stack(n_s), jnp.stack(m_s), jnp.stack(k_s), jnp.stack(v_s))
```

```python
import functools

import jax
import jax.numpy as jnp
from jax import lax
from jax.experimental import pallas as pl
from jax.experimental.pallas import tpu as pltpu

F32 = jnp.float32
BF16 = jnp.bfloat16

EPS = 1e-6
GATE_CAP = 15.0
LANES = 128
VMEM_LIMIT_BYTES = 56 * 1024 * 1024

ML_CHUNK = 128
SB_TQ = 128
SB_TK = 128
DEC_PAGES_PER_STEP = 4


def _params(*sem):
    return pltpu.CompilerParams(dimension_semantics=sem, vmem_limit_bytes=VMEM_LIMIT_BYTES)


def _row_tile(m, want):
    return want if m % want == 0 else m


def _rmsnorm_rows(x, g):
    return x * lax.rsqrt(jnp.mean(x * x, axis=-1, keepdims=True) + EPS) * g


def _log_sigmoid(x):
    return jnp.minimum(x, 0.0) - jnp.log1p(jnp.exp(-jnp.abs(x)))


def _softplus(x):
    return jnp.maximum(x, 0.0) + jnp.log1p(jnp.exp(-jnp.abs(x)))


def _split_bf16(x):
    hi = x.astype(BF16)
    lo = (x - hi.astype(F32)).astype(BF16)
    return hi, lo


def _dot(a, b):
    return jnp.dot(a, b, preferred_element_type=F32)


def _dot_nt(a, b):
    return lax.dot_general(a, b, (((1,), (1,)), ((), ())), preferred_element_type=F32)


def _dot_tn(a, b):
    return lax.dot_general(a, b, (((0,), (0,)), ((), ())), preferred_element_type=F32)


def _norm_matmul_kernel(*refs, seg_tiles, has_gate):
    x_ref, g_ref, w_ref = refs[:3]
    pos = 3
    wg_ref = None
    if has_gate:
        wg_ref = refs[pos]
        pos += 1
    n_seg = len(seg_tiles)
    out_refs = refs[pos:pos + n_seg]
    pos += n_seg
    gate_ref = None
    if has_gate:
        gate_ref = refs[pos]
        pos += 1
    h_ref = refs[pos]

    j = pl.program_id(1)

    @pl.when(j == 0)
    def _():
        h_ref[...] = _rmsnorm_rows(x_ref[...], g_ref[...]).astype(BF16)
        if has_gate:
            gate_ref[...] = _dot(h_ref[...], wg_ref[...])

    acc = _dot(h_ref[...], w_ref[...])
    lo = 0
    for out_ref, nt in zip(out_refs, seg_tiles):
        @pl.when((j >= lo) & (j < lo + nt))
        def _(out_ref=out_ref):
            out_ref[...] = acc.astype(out_ref.dtype)
        lo += nt


def norm_matmul(x, g, w, segs, *, w_gate=None, tm=512, tn=1024):
    m, k = x.shape
    tm = _row_tile(m, tm)
    seg_tiles = tuple(n // tn for n, _ in segs)
    assert all(n % tn == 0 for n, _ in segs) and m % tm == 0
    starts = []
    lo = 0
    for nt in seg_tiles:
        starts.append(lo)
        lo += nt
    n_tiles = lo

    in_specs = [
        pl.BlockSpec((tm, k), lambda i, j: (i, 0)),
        pl.BlockSpec((1, k), lambda i, j: (0, 0)),
        pl.BlockSpec((k, tn), lambda i, j: (0, j)),
    ]
    args = [x, g.reshape(1, k), w]
    if w_gate is not None:
        in_specs.append(pl.BlockSpec((k, LANES), lambda i, j: (0, 0)))
        args.append(w_gate)

    out_shape, out_specs = [], []
    for (n, dt), nt, st in zip(segs, seg_tiles, starts):
        out_shape.append(jax.ShapeDtypeStruct((m, n), dt))
        out_specs.append(pl.BlockSpec(
            (tm, tn), lambda i, j, st=st, nt=nt: (i, jnp.clip(j - st, 0, nt - 1))))
    if w_gate is not None:
        out_shape.append(jax.ShapeDtypeStruct((m, LANES), F32))
        out_specs.append(pl.BlockSpec((tm, LANES), lambda i, j: (i, 0)))

    return pl.pallas_call(
        functools.partial(_norm_matmul_kernel, seg_tiles=seg_tiles, has_gate=w_gate is not None),
        grid=(m // tm, n_tiles),
        in_specs=in_specs,
        out_specs=out_specs,
        out_shape=out_shape,
        scratch_shapes=[pltpu.VMEM((tm, k), BF16)],
        compiler_params=_params("parallel", "arbitrary"),
        name="norm_matmul",
    )(*args)


def _proj_residual_kernel(a_ref, w_ref, y_ref, out_ref):
    out_ref[...] = y_ref[...] + _dot(a_ref[...], w_ref[...])


def proj_residual(a, w, y, *, tm=512):
    m, k = a.shape
    n = w.shape[1]
    tm = _row_tile(m, tm)
    return pl.pallas_call(
        _proj_residual_kernel,
        grid=(m // tm,),
        in_specs=[
            pl.BlockSpec((tm, k), lambda i: (i, 0)),
            pl.BlockSpec((k, n), lambda i: (0, 0)),
            pl.BlockSpec((tm, n), lambda i: (i, 0)),
        ],
        out_specs=pl.BlockSpec((tm, n), lambda i: (i, 0)),
        out_shape=jax.ShapeDtypeStruct((m, n), F32),
        compiler_params=_params("parallel"),
        name="proj_residual",
    )(a, w, y)


def _mlp_kernel(*refs, final_norm):
    y_ref, g_ref, wu_ref, wd_ref = refs[:4]
    pos = 4
    gf_ref = None
    if final_norm:
        gf_ref = refs[pos]
        pos += 1
    out_ref, h_ref, acc_ref = refs[pos:pos + 3]

    f = pl.program_id(1)

    @pl.when(f == 0)
    def _():
        h_ref[...] = _rmsnorm_rows(y_ref[...], g_ref[...]).astype(BF16)
        acc_ref[...] = jnp.zeros_like(acc_ref)

    u = jnp.maximum(_dot(h_ref[...], wu_ref[...]), 0.0)
    acc_ref[...] += _dot((u * u).astype(BF16), wd_ref[...])

    @pl.when(f == pl.num_programs(1) - 1)
    def _():
        out = y_ref[...] + acc_ref[...]
        if final_norm:
            out = _rmsnorm_rows(out, gf_ref[...])
        out_ref[...] = out


def mlp_residual(y, g, w_up, w_down, final_g=None, *, tm=512, tf=512):
    m, d = y.shape
    d_ff = w_up.shape[1]
    tm = _row_tile(m, tm)
    assert d_ff % tf == 0
    in_specs = [
        pl.BlockSpec((tm, d), lambda i, f: (i, 0)),
        pl.BlockSpec((1, d), lambda i, f: (0, 0)),
        pl.BlockSpec((d, tf), lambda i, f: (0, f)),
        pl.BlockSpec((tf, d), lambda i, f: (f, 0)),
    ]
    args = [y, g.reshape(1, d), w_up, w_down]
    if final_g is not None:
        in_specs.append(pl.BlockSpec((1, d), lambda i, f: (0, 0)))
        args.append(final_g.reshape(1, d))
    return pl.pallas_call(
        functools.partial(_mlp_kernel, final_norm=final_g is not None),
        grid=(m // tm, d_ff // tf),
        in_specs=in_specs,
        out_specs=pl.BlockSpec((tm, d), lambda i, f: (i, 0)),
        out_shape=jax.ShapeDtypeStruct((m, d), F32),
        scratch_shapes=[pltpu.VMEM((tm, d), BF16), pltpu.VMEM((tm, d), F32)],
        compiler_params=_params("parallel", "arbitrary"),
        name="mlp_residual",
    )(*args)


def _gate_act(pre):
    capped = GATE_CAP * jnp.tanh(pre / GATE_CAP)
    return capped, _log_sigmoid(capped)


def _head_out(hcur, gh, o):
    hn = hcur * lax.rsqrt(jnp.mean(hcur * hcur, axis=-1, keepdims=True) + EPS)
    return hn * gh * jax.nn.sigmoid(o)


def _mlstm_chunk_kernel(q_ref, k_ref, v_ref, o_ref, gc_ref, gr_ref, bc_ref, br_ref, gh_ref,
                        tri_ref, tril_ref, hs_ref, c_ref, n_ref, m_ref, *, heads, dk, dv, chunk):
    c_idx = pl.program_id(1)

    @pl.when(c_idx == 0)
    def _():
        c_ref[...] = jnp.zeros_like(c_ref)
        n_ref[...] = jnp.zeros_like(n_ref)
        m_ref[...] = jnp.zeros_like(m_ref)

    k_scale = dk ** -0.5
    tri = tri_ref[...]
    tril = tril_ref[...]
    li_c, lf_c = _gate_act(gc_ref[...] + br_ref[...])
    li_r, lf_r = _gate_act(gr_ref[...] + bc_ref[...])
    hi, lo = _split_bf16(lf_c)
    b_c = _dot(tril, hi) + _dot(tril, lo)
    hi, lo = _split_bf16(lf_r)
    b_r = _dot(hi, tri) + _dot(lo, tri)

    row = lax.broadcasted_iota(jnp.int32, (chunk, chunk), 0)
    col = lax.broadcasted_iota(jnp.int32, (chunk, chunk), 1)
    causal = col <= row

    for h in range(heads):
        q = q_ref[:, h * dk:(h + 1) * dk]
        k = k_ref[:, h * dk:(h + 1) * dk] * k_scale
        v = v_ref[:, h * dv:(h + 1) * dv]
        i_col = li_c[:, h:h + 1]
        b_col = b_c[:, heads + h:heads + h + 1]
        i_row = li_r[h:h + 1, :]
        b_row = b_r[heads + h:heads + h + 1, :]
        m_st = m_ref[0, h]
        c_st = c_ref[0, h]
        n_st = n_ref[0, h]

        log_d = jnp.where(causal, (b_col - b_row) + i_row, -jnp.inf)
        log_inter = b_col + m_st
        m_row = jnp.maximum(log_inter, jnp.max(log_d, axis=-1, keepdims=True))
        s = _dot_nt(q, k) * jnp.exp(log_d - m_row)
        w_inter = jnp.exp(log_inter - m_row)
        qf = q.astype(F32)
        num = w_inter * _dot(q, c_st.astype(BF16)) + _dot(s.astype(BF16), v)
        den = (w_inter * jnp.sum(qf * n_st, axis=-1, keepdims=True)
               + jnp.sum(s, axis=-1, keepdims=True))
        hcur = num / jnp.maximum(jnp.abs(den), jnp.exp(-m_row))
        gh = gh_ref[:, h * dv:(h + 1) * dv]
        o = o_ref[:, h * dv:(h + 1) * dv].astype(F32)
        hs_ref[:, h * dv:(h + 1) * dv] = _head_out(hcur, gh, o).astype(hs_ref.dtype)

        b_last = b_col[chunk - 1:chunk, :]
        log_w = (b_last - b_col) + i_col
        m_new = jnp.maximum(b_last + m_st, jnp.max(log_w, axis=0, keepdims=True))
        w = jnp.exp(log_w - m_new)
        decay = jnp.exp(b_last + m_st - m_new)
        wv = (w * v.astype(F32)).astype(BF16)
        c_ref[0, h] = decay * c_st + _dot_tn(k, wv)
        n_ref[0, h] = decay * n_st + jnp.sum(w * k.astype(F32), axis=0, keepdims=True)
        m_ref[0, h] = m_new


def mlstm_prompt(qkvo, gates, b_gate, g_head, batch, seq, heads, dk, dv):
    chunk = ML_CHUNK if seq % ML_CHUNK == 0 else seq
    n_chunks = seq // chunk
    hk, hv = heads * dk, heads * dv
    assert hk % LANES == 0 and hv % hk == 0
    kv_ratio = hv // hk
    gates_r = gates[:, :2 * heads].T
    tri = (jnp.arange(chunk)[:, None] <= jnp.arange(chunk)[None, :]).astype(BF16)

    def rows(b, c):
        return b * n_chunks + c

    hs, c_f, n_f, m_f = pl.pallas_call(
        functools.partial(_mlstm_chunk_kernel, heads=heads, dk=dk, dv=dv, chunk=chunk),
        grid=(batch, n_chunks),
        in_specs=[
            pl.BlockSpec((chunk, hk), lambda b, c: (rows(b, c), 0)),
            pl.BlockSpec((chunk, hk), lambda b, c: (rows(b, c), 1)),
            pl.BlockSpec((chunk, hv), lambda b, c: (rows(b, c), 2 // kv_ratio)),
            pl.BlockSpec((chunk, hv), lambda b, c: (rows(b, c), 2 // kv_ratio + 1)),
            pl.BlockSpec((chunk, LANES), lambda b, c: (rows(b, c), 0)),
            pl.BlockSpec((2 * heads, chunk), lambda b, c: (0, rows(b, c))),
            pl.BlockSpec((2 * heads, 1), lambda b, c: (0, 0)),
            pl.BlockSpec((1, LANES), lambda b, c: (0, 0)),
            pl.BlockSpec((1, hv), lambda b, c: (0, 0)),
            pl.BlockSpec((chunk, chunk), lambda b, c: (0, 0)),
            pl.BlockSpec((chunk, chunk), lambda b, c: (0, 0)),
        ],
        out_specs=[
            pl.BlockSpec((chunk, hv), lambda b, c: (rows(b, c), 0)),
            pl.BlockSpec((1, heads, dk, dv), lambda b, c: (b, 0, 0, 0)),
            pl.BlockSpec((1, heads, 1, dk), lambda b, c: (b, 0, 0, 0)),
            pl.BlockSpec((1, heads, 1, 1), lambda b, c: (b, 0, 0, 0)),
        ],
        out_shape=[
            jax.ShapeDtypeStruct((batch * seq, hv), BF16),
            jax.ShapeDtypeStruct((batch, heads, dk, dv), F32),
            jax.ShapeDtypeStruct((batch, heads, 1, dk), F32),
            jax.ShapeDtypeStruct((batch, heads, 1, 1), F32),
        ],
        compiler_params=_params("parallel", "arbitrary"),
        name="mlstm_chunk",
    )(qkvo, qkvo, qkvo, qkvo, gates, gates_r,
      b_gate.reshape(2 * heads, 1),
      jnp.pad(b_gate, (0, LANES - 2 * heads)).reshape(1, LANES),
      g_head.reshape(1, hv), tri, tri.T)
    return hs, c_f, n_f.reshape(batch, heads, dk), m_f.reshape(batch, heads)


def _mlstm_step_kernel(qkvo_ref, gate_ref, br_ref, gh_ref, c_ref, n_ref, m_ref,
                       hs_ref, c_out, n_out, m_out, *, heads, dk, dv):
    hk, hv = heads * dk, heads * dv
    k_scale = dk ** -0.5
    li, lf = _gate_act(gate_ref[0] + br_ref[...])
    eye = (lax.broadcasted_iota(jnp.int32, (dk, dk), 0)
           == lax.broadcasted_iota(jnp.int32, (dk, dk), 1))

    def to_col(x_row):
        return jnp.sum(jnp.where(eye, x_row, 0.0), axis=-1, keepdims=True)

    for h in range(heads):
        q = qkvo_ref[0, :, h * dk:(h + 1) * dk].astype(F32)
        k = qkvo_ref[0, :, hk + h * dk:hk + (h + 1) * dk].astype(F32) * k_scale
        v = qkvo_ref[0, :, 2 * hk + h * dv:2 * hk + (h + 1) * dv].astype(F32)
        o = qkvo_ref[0, :, 2 * hk + hv + h * dv:2 * hk + hv + (h + 1) * dv].astype(F32)
        log_i = li[:, h:h + 1]
        log_f = lf[:, heads + h:heads + h + 1]
        c_st = c_ref[0, h]
        n_st = n_ref[0, h]
        m_st = m_ref[0, h]

        log_inter = log_f + m_st
        m_row = jnp.maximum(log_inter, log_i)
        s = jnp.sum(q * k, axis=-1, keepdims=True) * jnp.exp(log_i - m_row)
        w_inter = jnp.exp(log_inter - m_row)
        q_c = jnp.sum(to_col(q) * c_st, axis=0, keepdims=True)
        num = w_inter * q_c + s * v
        den = w_inter * jnp.sum(q * n_st, axis=-1, keepdims=True) + s
        hcur = num / jnp.maximum(jnp.abs(den), jnp.exp(-m_row))
        gh = gh_ref[:, h * dv:(h + 1) * dv]
        hs_ref[0, :, h * dv:(h + 1) * dv] = _head_out(hcur, gh, o).astype(hs_ref.dtype)

        m_new = jnp.maximum(log_f + m_st, log_i)
        w = jnp.exp(log_i - m_new)
        decay = jnp.exp(log_f + m_st - m_new)
        c_out[0, h] = decay * c_st + (w * to_col(k)) * v
        n_out[0, h] = decay * n_st + w * k
        m_out[0, h] = m_new


def mlstm_sample(qkvo, gates, b_gate, g_head, c0, n0, m0):
    batch, heads, dk, dv = c0.shape
    hk, hv = heads * dk, heads * dv
    width = qkvo.shape[1]
    hs, c_f, n_f, m_f = pl.pallas_call(
        functools.partial(_mlstm_step_kernel, heads=heads, dk=dk, dv=dv),
        grid=(batch,),
        in_specs=[
            pl.BlockSpec((1, 1, width), lambda b: (b, 0, 0)),
            pl.BlockSpec((1, 1, LANES), lambda b: (b, 0, 0)),
            pl.BlockSpec((1, LANES), lambda b: (0, 0)),
            pl.BlockSpec((1, hv), lambda b: (0, 0)),
            pl.BlockSpec((1, heads, dk, dv), lambda b: (b, 0, 0, 0)),
            pl.BlockSpec((1, heads, 1, dk), lambda b: (b, 0, 0, 0)),
            pl.BlockSpec((1, heads, 1, 1), lambda b: (b, 0, 0, 0)),
        ],
        out_specs=[
            pl.BlockSpec((1, 1, hv), lambda b: (b, 0, 0)),
            pl.BlockSpec((1, heads, dk, dv), lambda b: (b, 0, 0, 0)),
            pl.BlockSpec((1, heads, 1, dk), lambda b: (b, 0, 0, 0)),
            pl.BlockSpec((1, heads, 1, 1), lambda b: (b, 0, 0, 0)),
        ],
        out_shape=[
            jax.ShapeDtypeStruct((batch, 1, hv), BF16),
            jax.ShapeDtypeStruct((batch, heads, dk, dv), F32),
            jax.ShapeDtypeStruct((batch, heads, 1, dk), F32),
            jax.ShapeDtypeStruct((batch, heads, 1, 1), F32),
        ],
        compiler_params=_params("parallel"),
        name="mlstm_step",
    )(qkvo.reshape(batch, 1, width), gates.reshape(batch, 1, LANES),
      jnp.pad(b_gate, (0, LANES - 2 * heads)).reshape(1, LANES), g_head.reshape(1, hv),
      c0, n0.reshape(batch, heads, 1, dk), m0.reshape(batch, heads, 1, 1))
    return hs.reshape(batch, hv), c_f, n_f.reshape(batch, heads, dk), m_f.reshape(batch, heads)


def _suffix_matrix(tk):
    j = jnp.arange(tk)[:, None]
    s = jnp.arange(tk)[None, :]
    half = jnp.concatenate([(j > s).astype(BF16), jnp.ones((tk, tk), BF16)], axis=1)
    return jnp.concatenate([half, half], axis=0)


def _sb_block(z, mask, suffix, carry, tk):
    sp = _softplus(z)
    log_om = -sp if mask is None else jnp.where(mask, -sp, 0.0)
    hi, lo = _split_bf16(log_om)
    r = _dot(jnp.concatenate([hi, lo], axis=1), suffix)
    a = jnp.exp((z - sp) + (r[:, :tk] + carry))
    if mask is not None:
        a = jnp.where(mask, a, 0.0)
    return a, carry + r[:, tk:]


def _sb_prompt_kernel(bias_ref, q_ref, k_ref, v_ref, suffix_ref, o_ref, kb_ref, vb_ref,
                      *, tq, tk, scale):
    h = pl.program_id(1)
    i = pl.program_id(2)

    @pl.when(i == 0)
    def _():
        kb_ref[...] = k_ref[...].astype(BF16)
        vb_ref[...] = v_ref[...].astype(BF16)

    q = q_ref[...]
    bias = bias_ref[h]
    suffix = suffix_ref[...]

    def visit(j, carry, acc, mask):
        start = pl.multiple_of(j * tk, tk)
        kj = kb_ref[pl.ds(start, tk), :]
        vj = vb_ref[pl.ds(start, tk), :]
        z = _dot_nt(q, kj) * scale + bias
        a, carry = _sb_block(z, mask, suffix, carry, tk)
        return carry, acc + _dot(a.astype(BF16), vj)

    row = lax.broadcasted_iota(jnp.int32, (tq, tk), 0)
    col = lax.broadcasted_iota(jnp.int32, (tq, tk), 1)
    carry = jnp.zeros((tq, tk), F32)
    acc = jnp.zeros((tq, q.shape[1]), F32)
    carry, acc = visit(i, carry, acc, col < row)

    def body(jj, state):
        return visit(i - 1 - jj, state[0], state[1], None)

    carry, acc = lax.fori_loop(0, i, body, (carry, acc))
    o_ref[...] = acc.astype(o_ref.dtype)


def sb_prompt_attention(q, k, v, logit_bias, batch, seq, heads, hd):
    tq, tk = SB_TQ, SB_TK
    assert tq == tk and seq % tq == 0 and hd == LANES
    nq = seq // tq
    return pl.pallas_call(
        functools.partial(_sb_prompt_kernel, tq=tq, tk=tk, scale=hd ** -0.5),
        grid=(batch, heads, nq),
        in_specs=[
            pl.BlockSpec(memory_space=pltpu.SMEM),
            pl.BlockSpec((tq, hd), lambda b, h, i: (b * nq + i, h)),
            pl.BlockSpec((seq, hd), lambda b, h, i: (b, h)),
            pl.BlockSpec((seq, hd), lambda b, h, i: (b, h)),
            pl.BlockSpec((2 * tk, 2 * tk), lambda b, h, i: (0, 0)),
        ],
        out_specs=pl.BlockSpec((tq, hd), lambda b, h, i: (b * nq + i, h)),
        out_shape=jax.ShapeDtypeStruct((batch * seq, heads * hd), BF16),
        scratch_shapes=[pltpu.VMEM((seq, hd), BF16), pltpu.VMEM((seq, hd), BF16)],
        compiler_params=_params("parallel", "parallel", "arbitrary"),
        name="sb_prompt",
    )(logit_bias, q, k, v, _suffix_matrix(tk))


def _sb_decode_kernel(pt_ref, q_ref, kn_ref, vn_ref, bias_ref, suffix_ref, *rest,
                      pages, heads, hd, page, past_len, scale):
    k_refs = rest[:pages]
    v_refs = rest[pages:2 * pages]
    o_ref, qbd_ref, carry_ref, acc_ref = rest[2 * pages:]
    p = pl.program_id(1)
    width = heads * hd
    head_of_col = lax.broadcasted_iota(jnp.int32, (heads, width), 1) // hd
    own = head_of_col == lax.broadcasted_iota(jnp.int32, (heads, width), 0)
    bias = bias_ref[...]

    @pl.when(p == 0)
    def _():
        q_bd = jnp.where(own, q_ref[0].astype(F32), 0.0)
        qbd_ref[...] = q_bd.astype(BF16)
        k_new = kn_ref[0]
        v_new = vn_ref[0]
        z_new = jnp.sum(q_bd.astype(F32) * k_new, axis=-1, keepdims=True) * scale + bias
        visible = jnp.full((heads, 1), past_len, jnp.int32) < past_len
        sp = _softplus(z_new)
        log_om = jnp.where(visible, -sp, 0.0)
        a_new = jnp.where(visible, jnp.exp(z_new - sp), 0.0)
        carry_ref[...] = jnp.broadcast_to(log_om, carry_ref.shape)
        acc_ref[...] = a_new * v_new

    suffix = suffix_ref[...]
    carry = carry_ref[...]
    acc = acc_ref[...]
    for idx in range(pages):
        kp = k_refs[idx][0].astype(BF16)
        vp = v_refs[idx][0].astype(BF16)
        z = _dot_nt(qbd_ref[...], kp) * scale + bias
        a, carry = _sb_block(z, None, suffix, carry, page)
        acc = acc + _dot(a.astype(BF16), vp)
    carry_ref[...] = carry
    acc_ref[...] = acc

    @pl.when(p == pl.num_programs(1) - 1)
    def _():
        o_ref[0] = jnp.sum(jnp.where(own, acc, 0.0), axis=0, keepdims=True).astype(o_ref.dtype)


def sb_decode_attention(q, k_new, v_new, k_pool, v_pool, page_table, logit_bias, heads, hd):
    batch, width = q.shape
    page = k_pool.shape[1]
    n_pages = page_table.shape[1]
    past_len = n_pages * page
    pages = DEC_PAGES_PER_STEP if n_pages % DEC_PAGES_PER_STEP == 0 else 1
    assert page == LANES and hd == LANES
    steps = n_pages // pages

    def page_spec(idx):
        return pl.BlockSpec(
            (1, page, width),
            lambda b, p, pt, idx=idx: (pt[b, n_pages - 1 - (p * pages + idx)], 0, 0))

    vec = pl.BlockSpec((1, 1, width), lambda b, p, pt: (b, 0, 0))
    grid_spec = pltpu.PrefetchScalarGridSpec(
        num_scalar_prefetch=1,
        grid=(batch, steps),
        in_specs=[vec, vec, vec,
                  pl.BlockSpec((heads, 1), lambda b, p, pt: (0, 0)),
                  pl.BlockSpec((2 * page, 2 * page), lambda b, p, pt: (0, 0))]
                 + [page_spec(i) for i in range(pages)] * 2,
        out_specs=vec,
        scratch_shapes=[pltpu.VMEM((heads, width), BF16),
                        pltpu.VMEM((heads, page), F32),
                        pltpu.VMEM((heads, width), F32)],
    )
    out = pl.pallas_call(
        functools.partial(_sb_decode_kernel, pages=pages, heads=heads, hd=hd, page=page,
                          past_len=past_len, scale=hd ** -0.5),
        grid_spec=grid_spec,
        out_shape=jax.ShapeDtypeStruct((batch, 1, width), BF16),
        compiler_params=_params("parallel", "arbitrary"),
        name="sb_decode",
    )(page_table, q.reshape(batch, 1, width), k_new.reshape(batch, 1, width),
      v_new.reshape(batch, 1, width), logit_bias.reshape(heads, 1), _suffix_matrix(page),
      *([k_pool] * pages), *([v_pool] * pages))
    return out.reshape(batch, width)


def kernel(x_prompt, x_sample, state_C, state_n, state_m, cache_k, cache_v, page_table,
           norm_mix_g, norm_ffn_g, ml_w_in, ml_b_gate, ml_head_g, ml_w_out,
           sb_w_in, sb_logit_bias, sb_w_out, ffn_w_up, ffn_w_down, final_g):
    batch, seq, d = x_prompt.shape
    dec_batch, dec_seq, _ = x_sample.shape
    assert dec_seq == 1
    depth = norm_mix_g.shape[0]
    ml_heads, ml_dk, ml_dv = state_C.shape[2:]
    sb_heads, sb_hd = cache_k.shape[3:]
    hk, hv = ml_heads * ml_dk, ml_heads * ml_dv
    ml_main = 2 * hk + 2 * hv

    yp = x_prompt.reshape(batch * seq, d)
    ys = x_sample.reshape(dec_batch, d)
    outs = {name: [] for name in ("c_p", "n_p", "m_p", "k_p", "v_p", "c_s", "n_s", "m_s", "k_s", "v_s")}

    for i in range(depth):
        j = i // 2
        last = i == depth - 1
        if i % 2 == 0:
            w_in = ml_w_in[j].astype(BF16)
            w_gate = jnp.pad(w_in[:, ml_main:], ((0, 0), (0, LANES - 2 * ml_heads)))
            w_out = ml_w_out[j].astype(BF16)
            segs = [(ml_main, BF16)]
            qkvo, gates = norm_matmul(yp, norm_mix_g[i], w_in, segs, w_gate=w_gate)
            hs, cf, nf, mf = mlstm_prompt(qkvo, gates, ml_b_gate[j], ml_head_g[j],
                                          batch, seq, ml_heads, ml_dk, ml_dv)
            outs["c_p"].append(cf); outs["n_p"].append(nf); outs["m_p"].append(mf)
            yp = proj_residual(hs, w_out, yp)

            qkvo, gates = norm_matmul(ys, norm_mix_g[i], w_in, segs, w_gate=w_gate)
            hs, cf, nf, mf = mlstm_sample(qkvo, gates, ml_b_gate[j], ml_head_g[j],
                                          state_C[j], state_n[j], state_m[j])
            outs["c_s"].append(cf); outs["n_s"].append(nf); outs["m_s"].append(mf)
            ys = proj_residual(hs, w_out, ys)
        else:
            w_in = sb_w_in[j].astype(BF16)
            w_out = sb_w_out[j].astype(BF16)
            segs = [(d, BF16), (d, F32), (d, F32)]
            q, k, v = norm_matmul(yp, norm_mix_g[i], w_in, segs)
            o = sb_prompt_attention(q, k, v, sb_logit_bias[j], batch, seq, sb_heads, sb_hd)
            outs["k_p"].append(k.reshape(batch, seq, sb_heads, sb_hd))
            outs["v_p"].append(v.reshape(batch, seq, sb_heads, sb_hd))
            yp = proj_residual(o, w_out, yp)

            q, k, v = norm_matmul(ys, norm_mix_g[i], w_in, segs)
            n_pool, page = cache_k.shape[1:3]
            o = sb_decode_attention(q, k, v, cache_k[j].reshape(n_pool, page, d),
                                    cache_v[j].reshape(n_pool, page, d), page_table,
                                    sb_logit_bias[j], sb_heads, sb_hd)
            outs["k_s"].append(k.reshape(dec_batch, dec_seq, sb_heads, sb_hd))
            outs["v_s"].append(v.reshape(dec_batch, dec_seq, sb_heads, sb_hd))
            ys = proj_residual(o, w_out, ys)

        w_up = ffn_w_up[i].astype(BF16)
        w_down = ffn_w_down[i].astype(BF16)
        fg = final_g if last else None
        yp = mlp_residual(yp, norm_ffn_g[i], w_up, w_down, fg)
        ys = mlp_residual(ys, norm_ffn_g[i], w_up, w_down, fg)

    return (yp.reshape(batch, seq, d), ys.reshape(dec_batch, dec_seq, d),
            jnp.stack(outs["c_p"]), jnp.stack(outs["n_p"]), jnp.stack(outs["m_p"]),
            jnp.stack(outs["k_p"]), jnp.stack(outs["v_p"]),
            jnp.stack(outs["c_s"]), jnp.stack(outs["n_s"]), jnp.stack(outs["m_s"]),
            jnp.stack(outs["k_s"]), jnp.stack(outs["v_s"]))
```

```python
import functools

import jax
import jax.numpy as jnp
from jax import lax
from jax.experimental import pallas as pl
from jax.experimental.pallas import tpu as pltpu

F32 = jnp.float32
BF16 = jnp.bfloat16

EPS = 1e-6
GATE_CAP = 15.0
LANES = 128
VMEM_LIMIT_BYTES = 56 * 1024 * 1024

ML_CHUNK = 128
SB_TQ = 512
SB_TK = 128
DEC_PAGES_PER_STEP = 4


def _params(*sem):
    return pltpu.CompilerParams(dimension_semantics=sem, vmem_limit_bytes=VMEM_LIMIT_BYTES)


def _row_tile(m, want):
    return want if m % want == 0 else m


def _rmsnorm_rows(x, g):
    return x * lax.rsqrt(jnp.mean(x * x, axis=-1, keepdims=True) + EPS) * g


def _log_sigmoid(x):
    return jnp.minimum(x, 0.0) - jnp.log1p(jnp.exp(-jnp.abs(x)))


def _softplus(x):
    return jnp.maximum(x, 0.0) + jnp.log1p(jnp.exp(-jnp.abs(x)))


def _split_bf16(x):
    hi = x.astype(BF16)
    lo = (x - hi.astype(F32)).astype(BF16)
    return hi, lo


def _dot(a, b):
    return jnp.dot(a, b, preferred_element_type=F32)


def _dot_nt(a, b):
    return lax.dot_general(a, b, (((1,), (1,)), ((), ())), preferred_element_type=F32)


def _dot_tn(a, b):
    return lax.dot_general(a, b, (((0,), (0,)), ((), ())), preferred_element_type=F32)


def _norm_matmul_kernel(*refs, seg_tiles, has_gate):
    x_ref, g_ref, w_ref = refs[:3]
    pos = 3
    wg_ref = None
    if has_gate:
        wg_ref = refs[pos]
        pos += 1
    n_seg = len(seg_tiles)
    out_refs = refs[pos:pos + n_seg]
    pos += n_seg
    gate_ref = None
    if has_gate:
        gate_ref = refs[pos]
        pos += 1
    h_ref = refs[pos]

    j = pl.program_id(1)

    @pl.when(j == 0)
    def _():
        h_ref[...] = _rmsnorm_rows(x_ref[...], g_ref[...]).astype(BF16)
        if has_gate:
            gate_ref[...] = _dot(h_ref[...], wg_ref[...])

    acc = _dot(h_ref[...], w_ref[...])
    lo = 0
    for out_ref, nt in zip(out_refs, seg_tiles):
        @pl.when((j >= lo) & (j < lo + nt))
        def _(out_ref=out_ref):
            out_ref[...] = acc.astype(out_ref.dtype)
        lo += nt


def norm_matmul(x, g, w, segs, *, w_gate=None, tm=1024, tn=1024):
    m, k = x.shape
    tm = _row_tile(m, tm)
    seg_tiles = tuple(n // tn for n, _ in segs)
    assert all(n % tn == 0 for n, _ in segs) and m % tm == 0
    starts = []
    lo = 0
    for nt in seg_tiles:
        starts.append(lo)
        lo += nt
    n_tiles = lo

    in_specs = [
        pl.BlockSpec((tm, k), lambda i, j: (i, 0)),
        pl.BlockSpec((1, k), lambda i, j: (0, 0)),
        pl.BlockSpec((k, tn), lambda i, j: (0, j)),
    ]
    args = [x, g.reshape(1, k), w]
    if w_gate is not None:
        in_specs.append(pl.BlockSpec((k, LANES), lambda i, j: (0, 0)))
        args.append(w_gate)

    out_shape, out_specs = [], []
    for (n, dt), nt, st in zip(segs, seg_tiles, starts):
        out_shape.append(jax.ShapeDtypeStruct((m, n), dt))
        out_specs.append(pl.BlockSpec(
            (tm, tn), lambda i, j, st=st, nt=nt: (i, jnp.clip(j - st, 0, nt - 1))))
    if w_gate is not None:
        out_shape.append(jax.ShapeDtypeStruct((m, LANES), F32))
        out_specs.append(pl.BlockSpec((tm, LANES), lambda i, j: (i, 0)))

    return pl.pallas_call(
        functools.partial(_norm_matmul_kernel, seg_tiles=seg_tiles, has_gate=w_gate is not None),
        grid=(m // tm, n_tiles),
        in_specs=in_specs,
        out_specs=out_specs,
        out_shape=out_shape,
        scratch_shapes=[pltpu.VMEM((tm, k), BF16)],
        compiler_params=_params("parallel", "arbitrary"),
        name="norm_matmul",
    )(*args)


def _proj_residual_kernel(a_ref, w_ref, y_ref, out_ref):
    out_ref[...] = y_ref[...] + _dot(a_ref[...], w_ref[...])


def proj_residual(a, w, y, *, tm=512):
    m, k = a.shape
    n = w.shape[1]
    tm = _row_tile(m, tm)
    return pl.pallas_call(
        _proj_residual_kernel,
        grid=(m // tm,),
        in_specs=[
            pl.BlockSpec((tm, k), lambda i: (i, 0)),
            pl.BlockSpec((k, n), lambda i: (0, 0)),
            pl.BlockSpec((tm, n), lambda i: (i, 0)),
        ],
        out_specs=pl.BlockSpec((tm, n), lambda i: (i, 0)),
        out_shape=jax.ShapeDtypeStruct((m, n), F32),
        compiler_params=_params("parallel"),
        name="proj_residual",
    )(a, w, y)


def _mlp_kernel(*refs, final_norm):
    y_ref, g_ref, wu_ref, wd_ref = refs[:4]
    pos = 4
    gf_ref = None
    if final_norm:
        gf_ref = refs[pos]
        pos += 1
    out_ref, h_ref = refs[pos:pos + 2]

    f = pl.program_id(1)

    @pl.when(f == 0)
    def _():
        y = y_ref[...]
        h_ref[...] = _rmsnorm_rows(y, g_ref[...]).astype(BF16)
        out_ref[...] = y

    u = jnp.maximum(_dot(h_ref[...], wu_ref[...]), 0.0)
    out_ref[...] += _dot((u * u).astype(BF16), wd_ref[...])

    if final_norm:
        @pl.when(f == pl.num_programs(1) - 1)
        def _():
            out_ref[...] = _rmsnorm_rows(out_ref[...], gf_ref[...])


def mlp_residual(y, g, w_up, w_down, final_g=None, *, tm=1024, tf=512):
    m, d = y.shape
    d_ff = w_up.shape[1]
    tm = _row_tile(m, tm)
    assert d_ff % tf == 0
    in_specs = [
        pl.BlockSpec((tm, d), lambda i, f: (i, 0)),
        pl.BlockSpec((1, d), lambda i, f: (0, 0)),
        pl.BlockSpec((d, tf), lambda i, f: (0, f)),
        pl.BlockSpec((tf, d), lambda i, f: (f, 0)),
    ]
    args = [y, g.reshape(1, d), w_up, w_down]
    if final_g is not None:
        in_specs.append(pl.BlockSpec((1, d), lambda i, f: (0, 0)))
        args.append(final_g.reshape(1, d))
    return pl.pallas_call(
        functools.partial(_mlp_kernel, final_norm=final_g is not None),
        grid=(m // tm, d_ff // tf),
        in_specs=in_specs,
        out_specs=pl.BlockSpec((tm, d), lambda i, f: (i, 0)),
        out_shape=jax.ShapeDtypeStruct((m, d), F32),
        scratch_shapes=[pltpu.VMEM((tm, d), BF16)],
        compiler_params=_params("parallel", "arbitrary"),
        name="mlp_residual",
    )(*args)


def _gate_act(pre):
    capped = GATE_CAP * jnp.tanh(pre / GATE_CAP)
    return capped, _log_sigmoid(capped)


def _head_out(hcur, gh, o):
    hn = hcur * lax.rsqrt(jnp.mean(hcur * hcur, axis=-1, keepdims=True) + EPS)
    return hn * gh * jax.nn.sigmoid(o)


def _mlstm_chunk_kernel(q_ref, k_ref, v_ref, o_ref, gc_ref, gr_ref, bc_ref, br_ref, gh_ref,
                        tri_ref, tril_ref, hs_ref, c_ref, n_ref, m_ref, *, heads, dk, dv, chunk):
    c_idx = pl.program_id(1)

    @pl.when(c_idx == 0)
    def _():
        c_ref[...] = jnp.zeros_like(c_ref)
        n_ref[...] = jnp.zeros_like(n_ref)
        m_ref[...] = jnp.zeros_like(m_ref)

    k_scale = dk ** -0.5
    tri = tri_ref[...]
    tril = tril_ref[...]
    li_c, lf_c = _gate_act(gc_ref[...] + br_ref[...])
    li_r, lf_r = _gate_act(gr_ref[...] + bc_ref[...])
    hi, lo = _split_bf16(lf_c)
    b_c = _dot(tril, hi) + _dot(tril, lo)
    hi, lo = _split_bf16(lf_r)
    b_r = _dot(hi, tri) + _dot(lo, tri)

    row = lax.broadcasted_iota(jnp.int32, (chunk, chunk), 0)
    col = lax.broadcasted_iota(jnp.int32, (chunk, chunk), 1)
    causal = col <= row

    for h in range(heads):
        q = q_ref[:, h * dk:(h + 1) * dk]
        k = k_ref[:, h * dk:(h + 1) * dk] * k_scale
        v = v_ref[:, h * dv:(h + 1) * dv]
        i_col = li_c[:, h:h + 1]
        b_col = b_c[:, heads + h:heads + h + 1]
        i_row = li_r[h:h + 1, :]
        b_row = b_r[heads + h:heads + h + 1, :]
        m_st = m_ref[0, h]
        c_st = c_ref[0, h]
        n_st = n_ref[0, h]

        log_d = jnp.where(causal, (b_col - b_row) + i_row, -jnp.inf)
        log_inter = b_col + m_st
        m_row = jnp.maximum(log_inter, jnp.max(log_d, axis=-1, keepdims=True))
        s = _dot_nt(q, k) * jnp.exp(log_d - m_row)
        w_inter = jnp.exp(log_inter - m_row)
        qf = q.astype(F32)
        num = w_inter * _dot(q, c_st.astype(BF16)) + _dot(s.astype(BF16), v)
        den = (w_inter * jnp.sum(qf * n_st, axis=-1, keepdims=True)
               + jnp.sum(s, axis=-1, keepdims=True))
        hcur = num / jnp.maximum(jnp.abs(den), jnp.exp(-m_row))
        gh = gh_ref[:, h * dv:(h + 1) * dv]
        o = o_ref[:, h * dv:(h + 1) * dv].astype(F32)
        hs_ref[:, h * dv:(h + 1) * dv] = _head_out(hcur, gh, o).astype(hs_ref.dtype)

        b_last = b_col[chunk - 1:chunk, :]
        log_w = (b_last - b_col) + i_col
        m_new = jnp.maximum(b_last + m_st, jnp.max(log_w, axis=0, keepdims=True))
        w = jnp.exp(log_w - m_new)
        decay = jnp.exp(b_last + m_st - m_new)
        wv = (w * v.astype(F32)).astype(BF16)
        c_ref[0, h] = decay * c_st + _dot_tn(k, wv)
        n_ref[0, h] = decay * n_st + jnp.sum(w * k.astype(F32), axis=0, keepdims=True)
        m_ref[0, h] = m_new


def mlstm_prompt(qkvo, gates, b_gate, g_head, batch, seq, heads, dk, dv):
    chunk = ML_CHUNK if seq % ML_CHUNK == 0 else seq
    n_chunks = seq // chunk
    hk, hv = heads * dk, heads * dv
    assert hk % LANES == 0 and hv % hk == 0
    kv_ratio = hv // hk
    gates_r = gates[:, :2 * heads].T
    tri = (jnp.arange(chunk)[:, None] <= jnp.arange(chunk)[None, :]).astype(BF16)

    def rows(b, c):
        return b * n_chunks + c

    hs, c_f, n_f, m_f = pl.pallas_call(
        functools.partial(_mlstm_chunk_kernel, heads=heads, dk=dk, dv=dv, chunk=chunk),
        grid=(batch, n_chunks),
        in_specs=[
            pl.BlockSpec((chunk, hk), lambda b, c: (rows(b, c), 0)),
            pl.BlockSpec((chunk, hk), lambda b, c: (rows(b, c), 1)),
            pl.BlockSpec((chunk, hv), lambda b, c: (rows(b, c), 2 // kv_ratio)),
            pl.BlockSpec((chunk, hv), lambda b, c: (rows(b, c), 2 // kv_ratio + 1)),
            pl.BlockSpec((chunk, LANES), lambda b, c: (rows(b, c), 0)),
            pl.BlockSpec((2 * heads, chunk), lambda b, c: (0, rows(b, c))),
            pl.BlockSpec((2 * heads, 1), lambda b, c: (0, 0)),
            pl.BlockSpec((1, LANES), lambda b, c: (0, 0)),
            pl.BlockSpec((1, hv), lambda b, c: (0, 0)),
            pl.BlockSpec((chunk, chunk), lambda b, c: (0, 0)),
            pl.BlockSpec((chunk, chunk), lambda b, c: (0, 0)),
        ],
        out_specs=[
            pl.BlockSpec((chunk, hv), lambda b, c: (rows(b, c), 0)),
            pl.BlockSpec((1, heads, dk, dv), lambda b, c: (b, 0, 0, 0)),
            pl.BlockSpec((1, heads, 1, dk), lambda b, c: (b, 0, 0, 0)),
            pl.BlockSpec((1, heads, 1, 1), lambda b, c: (b, 0, 0, 0)),
        ],
        out_shape=[
            jax.ShapeDtypeStruct((batch * seq, hv), BF16),
            jax.ShapeDtypeStruct((batch, heads, dk, dv), F32),
            jax.ShapeDtypeStruct((batch, heads, 1, dk), F32),
            jax.ShapeDtypeStruct((batch, heads, 1, 1), F32),
        ],
        compiler_params=_params("parallel", "arbitrary"),
        name="mlstm_chunk",
    )(qkvo, qkvo, qkvo, qkvo, gates, gates_r,
      b_gate.reshape(2 * heads, 1),
      jnp.pad(b_gate, (0, LANES - 2 * heads)).reshape(1, LANES),
      g_head.reshape(1, hv), tri, tri.T)
    return hs, c_f, n_f.reshape(batch, heads, dk), m_f.reshape(batch, heads)


def _mlstm_step_kernel(qkvo_ref, gate_ref, br_ref, gh_ref, c_ref, n_ref, m_ref,
                       hs_ref, c_out, n_out, m_out, *, heads, dk, dv):
    hk, hv = heads * dk, heads * dv
    k_scale = dk ** -0.5
    li, lf = _gate_act(gate_ref[0] + br_ref[...])
    eye = (lax.broadcasted_iota(jnp.int32, (dk, dk), 0)
           == lax.broadcasted_iota(jnp.int32, (dk, dk), 1))

    def to_col(x_row):
        return jnp.sum(jnp.where(eye, x_row, 0.0), axis=-1, keepdims=True)

    for h in range(heads):
        q = qkvo_ref[0, :, h * dk:(h + 1) * dk].astype(F32)
        k = qkvo_ref[0, :, hk + h * dk:hk + (h + 1) * dk].astype(F32) * k_scale
        v = qkvo_ref[0, :, 2 * hk + h * dv:2 * hk + (h + 1) * dv].astype(F32)
        o = qkvo_ref[0, :, 2 * hk + hv + h * dv:2 * hk + hv + (h + 1) * dv].astype(F32)
        log_i = li[:, h:h + 1]
        log_f = lf[:, heads + h:heads + h + 1]
        c_st = c_ref[0, h]
        n_st = n_ref[0, h]
        m_st = m_ref[0, h]

        log_inter = log_f + m_st
        m_row = jnp.maximum(log_inter, log_i)
        s = jnp.sum(q * k, axis=-1, keepdims=True) * jnp.exp(log_i - m_row)
        w_inter = jnp.exp(log_inter - m_row)
        q_c = jnp.sum(to_col(q) * c_st, axis=0, keepdims=True)
        num = w_inter * q_c + s * v
        den = w_inter * jnp.sum(q * n_st, axis=-1, keepdims=True) + s
        hcur = num / jnp.maximum(jnp.abs(den), jnp.exp(-m_row))
        gh = gh_ref[:, h * dv:(h + 1) * dv]
        hs_ref[0, :, h * dv:(h + 1) * dv] = _head_out(hcur, gh, o).astype(hs_ref.dtype)

        m_new = jnp.maximum(log_f + m_st, log_i)
        w = jnp.exp(log_i - m_new)
        decay = jnp.exp(log_f + m_st - m_new)
        c_out[0, h] = decay * c_st + (w * to_col(k)) * v
        n_out[0, h] = decay * n_st + w * k
        m_out[0, h] = m_new


def mlstm_sample(qkvo, gates, b_gate, g_head, c0, n0, m0):
    batch, heads, dk, dv = c0.shape
    hk, hv = heads * dk, heads * dv
    width = qkvo.shape[1]
    hs, c_f, n_f, m_f = pl.pallas_call(
        functools.partial(_mlstm_step_kernel, heads=heads, dk=dk, dv=dv),
        grid=(batch,),
        in_specs=[
            pl.BlockSpec((1, 1, width), lambda b: (b, 0, 0)),
            pl.BlockSpec((1, 1, LANES), lambda b: (b, 0, 0)),
            pl.BlockSpec((1, LANES), lambda b: (0, 0)),
            pl.BlockSpec((1, hv), lambda b: (0, 0)),
            pl.BlockSpec((1, heads, dk, dv), lambda b: (b, 0, 0, 0)),
            pl.BlockSpec((1, heads, 1, dk), lambda b: (b, 0, 0, 0)),
            pl.BlockSpec((1, heads, 1, 1), lambda b: (b, 0, 0, 0)),
        ],
        out_specs=[
            pl.BlockSpec((1, 1, hv), lambda b: (b, 0, 0)),
            pl.BlockSpec((1, heads, dk, dv), lambda b: (b, 0, 0, 0)),
            pl.BlockSpec((1, heads, 1, dk), lambda b: (b, 0, 0, 0)),
            pl.BlockSpec((1, heads, 1, 1), lambda b: (b, 0, 0, 0)),
        ],
        out_shape=[
            jax.ShapeDtypeStruct((batch, 1, hv), BF16),
            jax.ShapeDtypeStruct((batch, heads, dk, dv), F32),
            jax.ShapeDtypeStruct((batch, heads, 1, dk), F32),
            jax.ShapeDtypeStruct((batch, heads, 1, 1), F32),
        ],
        compiler_params=_params("parallel"),
        name="mlstm_step",
    )(qkvo.reshape(batch, 1, width), gates.reshape(batch, 1, LANES),
      jnp.pad(b_gate, (0, LANES - 2 * heads)).reshape(1, LANES), g_head.reshape(1, hv),
      c0, n0.reshape(batch, heads, 1, dk), m0.reshape(batch, heads, 1, 1))
    return hs.reshape(batch, hv), c_f, n_f.reshape(batch, heads, dk), m_f.reshape(batch, heads)


def _suffix_matrix(tk):
    j = jnp.arange(tk)[:, None]
    s = jnp.arange(tk)[None, :]
    half = jnp.concatenate([(j >= s).astype(BF16), jnp.ones((tk, tk), BF16)], axis=1)
    return jnp.concatenate([half, half], axis=0)


def _sb_block(z, mask, suffix, carry, tk):
    n = z.shape[1] // tk
    neg = -z
    log_om = jnp.minimum(neg, 0.0) - jnp.log(1.0 + jnp.exp(jnp.minimum(z, neg)))
    args = [None] * n
    for c in range(n - 1, -1, -1):
        cols = slice(c * tk, (c + 1) * tk)
        part = log_om[:, cols]
        if mask is not None and c == n - 1:
            part = jnp.where(mask, part, 0.0)
        hi, lo = _split_bf16(part)
        r = _dot(jnp.concatenate([hi, lo], axis=1), suffix)
        args[c] = (z[:, cols] + carry) + r[:, :tk]
        if mask is not None and c == n - 1:
            args[c] = jnp.where(mask, args[c], -jnp.inf)
        carry = carry + r[:, tk:]
    return jnp.exp(jnp.concatenate(args, axis=1)), carry


def _sb_prompt_kernel(bias_ref, q_ref, k_ref, v_ref, suffix_ref, o_ref, kb_ref, vb_ref,
                      *, tq, tk, nq, scale):
    h = pl.program_id(1)
    i = pl.program_id(2)
    n_sub = tq // tk

    hd = q_ref.shape[1]

    @pl.when(i == 0)
    def _():
        kb_ref[:, :hd] = (k_ref[...] * scale).astype(BF16)
        rows = kb_ref.shape[0]
        b0 = jnp.full((rows, hd), bias_ref[h], F32)
        p0 = b0.astype(BF16).astype(F32)
        b1 = b0 - p0
        p1 = b1.astype(BF16).astype(F32)
        p2 = b1 - p1
        lane = lax.broadcasted_iota(jnp.int32, (rows, hd), 1)
        pieces = jnp.where(lane == 0, p0, jnp.where(lane == 1, p1, jnp.where(lane == 2, p2, 0.0)))
        kb_ref[:, hd:] = pieces.astype(BF16)
        vb_ref[...] = v_ref[...].astype(BF16)

    suffix = suffix_ref[...]
    strict = (lax.broadcasted_iota(jnp.int32, (tk, tk), 1)
              < lax.broadcasted_iota(jnp.int32, (tk, tk), 0))
    q_aug = jnp.concatenate([q_ref[...], jnp.ones((tq, hd), BF16)], axis=1)

    def logits(start):
        return _dot_nt(q_aug, kb_ref[start:start + tq, :])

    def own_keys(z, start):
        weights, carries = [], []
        for r in range(n_sub):
            a, carry = _sb_block(z[r * tk:(r + 1) * tk, :(r + 1) * tk], strict, suffix,
                                 jnp.zeros((tk, tk), F32), tk)
            if r < n_sub - 1:
                a = jnp.concatenate([a, jnp.zeros((tk, (n_sub - 1 - r) * tk), F32)], axis=1)
            weights.append(a)
            carries.append(carry)
        a = jnp.concatenate(weights, axis=0).astype(BF16)
        return jnp.concatenate(carries, axis=0), _dot(a, vb_ref[start:start + tq, :])

    def older_keys(z, start, carry, acc):
        a, carry = _sb_block(z, None, suffix, carry, tk)
        return carry, acc + _dot(a.astype(BF16), vb_ref[start:start + tq, :])

    for blk in range(nq):
        @pl.when(i == blk)
        def _(blk=blk):
            starts = [(blk - b) * tq for b in range(blk + 1)]
            z_next = logits(starts[0])
            carry = acc = None
            for b, start in enumerate(starts):
                z = z_next
                if b + 1 < len(starts):
                    z_next = logits(starts[b + 1])
                if b == 0:
                    carry, acc = own_keys(z, start)
                else:
                    carry, acc = older_keys(z, start, carry, acc)
            o_ref[...] = acc.astype(o_ref.dtype)


def sb_prompt_attention(q, k, v, logit_bias, batch, seq, heads, hd):
    tk = SB_TK
    tq = SB_TQ if seq % SB_TQ == 0 else tk
    assert tq % tk == 0 and seq % tq == 0 and hd == LANES
    nq = seq // tq
    return pl.pallas_call(
        functools.partial(_sb_prompt_kernel, tq=tq, tk=tk, nq=nq, scale=hd ** -0.5),
        grid=(batch, heads, nq),
        in_specs=[
            pl.BlockSpec(memory_space=pltpu.SMEM),
            pl.BlockSpec((tq, hd), lambda b, h, i: (b * nq + i, h)),
            pl.BlockSpec((seq, hd), lambda b, h, i: (b, h)),
            pl.BlockSpec((seq, hd), lambda b, h, i: (b, h)),
            pl.BlockSpec((2 * tk, 2 * tk), lambda b, h, i: (0, 0)),
        ],
        out_specs=pl.BlockSpec((tq, hd), lambda b, h, i: (b * nq + i, h)),
        out_shape=jax.ShapeDtypeStruct((batch * seq, heads * hd), BF16),
        scratch_shapes=[pltpu.VMEM((seq, 2 * hd), BF16), pltpu.VMEM((seq, hd), BF16)],
        compiler_params=_params("parallel", "parallel", "arbitrary"),
        name="sb_prompt",
    )(logit_bias, q, k, v, _suffix_matrix(tk))


def _sb_decode_kernel(pt_ref, q_ref, kn_ref, vn_ref, bias_ref, suffix_ref, *rest,
                      pages, heads, hd, page, past_len, scale):
    k_refs = rest[:pages]
    v_refs = rest[pages:2 * pages]
    o_ref, qbd_ref, carry_ref, acc_ref = rest[2 * pages:]
    p = pl.program_id(1)
    width = heads * hd
    head_of_col = lax.broadcasted_iota(jnp.int32, (heads, width), 1) // hd
    own = head_of_col == lax.broadcasted_iota(jnp.int32, (heads, width), 0)
    bias = bias_ref[...]

    @pl.when(p == 0)
    def _():
        q_bd = jnp.where(own, q_ref[0].astype(F32), 0.0)
        qbd_ref[...] = q_bd.astype(BF16)
        k_new = kn_ref[0]
        v_new = vn_ref[0]
        z_new = jnp.sum(q_bd.astype(F32) * k_new, axis=-1, keepdims=True) * scale + bias
        visible = jnp.full((heads, 1), past_len, jnp.int32) < past_len
        sp = _softplus(z_new)
        log_om = jnp.where(visible, -sp, 0.0)
        a_new = jnp.where(visible, jnp.exp(z_new - sp), 0.0)
        carry_ref[...] = jnp.broadcast_to(log_om, carry_ref.shape)
        acc_ref[...] = a_new * v_new

    def token_rows(ref):
        return jnp.concatenate(
            [ref[0, pl.ds(hh, page, stride=heads), :] for hh in range(heads)], axis=1)

    suffix = suffix_ref[...]
    carry = carry_ref[...]
    acc = acc_ref[...]
    for idx in range(pages):
        kp = token_rows(k_refs[idx]).astype(BF16)
        vp = token_rows(v_refs[idx]).astype(BF16)
        z = _dot_nt(qbd_ref[...], kp) * scale + bias
        a, carry = _sb_block(z, None, suffix, carry, page)
        acc = acc + _dot(a.astype(BF16), vp)
    carry_ref[...] = carry
    acc_ref[...] = acc

    @pl.when(p == pl.num_programs(1) - 1)
    def _():
        o_ref[0] = jnp.sum(jnp.where(own, acc, 0.0), axis=0, keepdims=True).astype(o_ref.dtype)


def sb_decode_attention(q, k_new, v_new, k_pool, v_pool, first_page, page_table, logit_bias,
                        heads, hd):
    batch, width = q.shape
    page = k_pool.shape[1] // heads
    n_pages = page_table.shape[1]
    past_len = n_pages * page
    pages = DEC_PAGES_PER_STEP if n_pages % DEC_PAGES_PER_STEP == 0 else 1
    assert page == LANES and hd == LANES
    steps = n_pages // pages

    def page_spec(idx):
        return pl.BlockSpec(
            (1, page * heads, hd),
            lambda b, p, pt, idx=idx: (first_page + pt[b, n_pages - 1 - (p * pages + idx)], 0, 0))

    vec = pl.BlockSpec((1, 1, width), lambda b, p, pt: (b, 0, 0))
    grid_spec = pltpu.PrefetchScalarGridSpec(
        num_scalar_prefetch=1,
        grid=(batch, steps),
        in_specs=[vec, vec, vec,
                  pl.BlockSpec((heads, 1), lambda b, p, pt: (0, 0)),
                  pl.BlockSpec((2 * page, 2 * page), lambda b, p, pt: (0, 0))]
                 + [page_spec(i) for i in range(pages)] * 2,
        out_specs=vec,
        scratch_shapes=[pltpu.VMEM((heads, width), BF16),
                        pltpu.VMEM((heads, page), F32),
                        pltpu.VMEM((heads, width), F32)],
    )
    out = pl.pallas_call(
        functools.partial(_sb_decode_kernel, pages=pages, heads=heads, hd=hd, page=page,
                          past_len=past_len, scale=hd ** -0.5),
        grid_spec=grid_spec,
        out_shape=jax.ShapeDtypeStruct((batch, 1, width), BF16),
        compiler_params=_params("parallel", "arbitrary"),
        name="sb_decode",
    )(page_table, q.reshape(batch, 1, width), k_new.reshape(batch, 1, width),
      v_new.reshape(batch, 1, width), logit_bias.reshape(heads, 1), _suffix_matrix(page),
      *([k_pool] * pages), *([v_pool] * pages))
    return out.reshape(batch, width)


def kernel(x_prompt, x_sample, state_C, state_n, state_m, cache_k, cache_v, page_table,
           norm_mix_g, norm_ffn_g, ml_w_in, ml_b_gate, ml_head_g, ml_w_out,
           sb_w_in, sb_logit_bias, sb_w_out, ffn_w_up, ffn_w_down, final_g):
    batch, seq, d = x_prompt.shape
    dec_batch, dec_seq, _ = x_sample.shape
    assert dec_seq == 1
    depth = norm_mix_g.shape[0]
    ml_heads, ml_dk, ml_dv = state_C.shape[2:]
    sb_heads, sb_hd = cache_k.shape[3:]
    hk, hv = ml_heads * ml_dk, ml_heads * ml_dv
    ml_main = 2 * hk + 2 * hv

    yp = x_prompt.reshape(batch * seq, d)
    ys = x_sample.reshape(dec_batch, d)
    outs = {name: [] for name in ("c_p", "n_p", "m_p", "k_p", "v_p", "c_s", "n_s", "m_s", "k_s", "v_s")}

    for i in range(depth):
        j = i // 2
        last = i == depth - 1
        if i % 2 == 0:
            w_in = ml_w_in[j].astype(BF16)
            w_gate = jnp.pad(w_in[:, ml_main:], ((0, 0), (0, LANES - 2 * ml_heads)))
            w_out = ml_w_out[j].astype(BF16)
            segs = [(ml_main, BF16)]
            qkvo, gates = norm_matmul(yp, norm_mix_g[i], w_in, segs, w_gate=w_gate)
            hs, cf, nf, mf = mlstm_prompt(qkvo, gates, ml_b_gate[j], ml_head_g[j],
                                          batch, seq, ml_heads, ml_dk, ml_dv)
            outs["c_p"].append(cf); outs["n_p"].append(nf); outs["m_p"].append(mf)
            yp = proj_residual(hs, w_out, yp)

            qkvo, gates = norm_matmul(ys, norm_mix_g[i], w_in, segs, w_gate=w_gate)
            hs, cf, nf, mf = mlstm_sample(qkvo, gates, ml_b_gate[j], ml_head_g[j],
                                          state_C[j], state_n[j], state_m[j])
            outs["c_s"].append(cf); outs["n_s"].append(nf); outs["m_s"].append(mf)
            ys = proj_residual(hs, w_out, ys)
        else:
            w_in = sb_w_in[j].astype(BF16)
            w_out = sb_w_out[j].astype(BF16)
            segs = [(d, BF16), (d, F32), (d, F32)]
            q, k, v = norm_matmul(yp, norm_mix_g[i], w_in, segs)
            o = sb_prompt_attention(q, k, v, sb_logit_bias[j], batch, seq, sb_heads, sb_hd)
            outs["k_p"].append(k.reshape(batch, seq, sb_heads, sb_hd))
            outs["v_p"].append(v.reshape(batch, seq, sb_heads, sb_hd))
            yp = proj_residual(o, w_out, yp)

            q, k, v = norm_matmul(ys, norm_mix_g[i], w_in, segs)
            n_layers, n_pool, page = cache_k.shape[:3]
            pool_shape = (n_layers * n_pool, page * sb_heads, sb_hd)
            o = sb_decode_attention(q, k, v, cache_k.reshape(pool_shape), cache_v.reshape(pool_shape),
                                    j * n_pool, page_table, sb_logit_bias[j], sb_heads, sb_hd)
            outs["k_s"].append(k.reshape(dec_batch, dec_seq, sb_heads, sb_hd))
            outs["v_s"].append(v.reshape(dec_batch, dec_seq, sb_heads, sb_hd))
            ys = proj_residual(o, w_out, ys)

        w_up = ffn_w_up[i].astype(BF16)
        w_down = ffn_w_down[i].astype(BF16)
        fg = final_g if last else None
        yp = mlp_residual(yp, norm_ffn_g[i], w_up, w_down, fg)
        ys = mlp_residual(ys, norm_ffn_g[i], w_up, w_down, fg)

    return (yp.reshape(batch, seq, d), ys.reshape(dec_batch, dec_seq, d),
            jnp.stack(outs["c_p"]), jnp.stack(outs["n_p"]), jnp.stack(outs["m_p"]),
            jnp.stack(outs["k_p"]), jnp.stack(outs["v_p"]),
            jnp.stack(outs["c_s"]), jnp.stack(outs["n_s"]), jnp.stack(outs["m_s"]),
            jnp.stack(outs["k_s"]), jnp.stack(outs["v_s"]))
```

```python
import functools

import jax
import jax.numpy as jnp
from jax import lax
from jax.experimental import pallas as pl
from jax.experimental.pallas import tpu as pltpu

F32 = jnp.float32
BF16 = jnp.bfloat16

EPS = 1e-6
GATE_CAP = 15.0
LANES = 128
VMEM_LIMIT_BYTES = 56 * 1024 * 1024

ML_CHUNK = 128
SB_TQ = 512
SB_TK = 128
DEC_PAGES_PER_STEP = 4
MLP_TM = 1024
MLP_TF = 512


def _params(*sem):
    return pltpu.CompilerParams(dimension_semantics=sem, vmem_limit_bytes=VMEM_LIMIT_BYTES)


def _row_tile(m, want):
    return want if m % want == 0 else m


def _rmsnorm_rows(x, g):
    return x * lax.rsqrt(jnp.mean(x * x, axis=-1, keepdims=True) + EPS) * g


def _log_sigmoid(x):
    return jnp.minimum(x, 0.0) - jnp.log1p(jnp.exp(-jnp.abs(x)))


def _softplus(x):
    return jnp.maximum(x, 0.0) + jnp.log1p(jnp.exp(-jnp.abs(x)))


def _split_bf16(x):
    hi = x.astype(BF16)
    lo = (x - hi.astype(F32)).astype(BF16)
    return hi, lo


def _dot(a, b):
    return jnp.dot(a, b, preferred_element_type=F32)


def _dot_nt(a, b):
    return lax.dot_general(a, b, (((1,), (1,)), ((), ())), preferred_element_type=F32)


def _dot_tn(a, b):
    return lax.dot_general(a, b, (((0,), (0,)), ((), ())), preferred_element_type=F32)


def _norm_matmul_kernel(*refs, seg_tiles, has_gate):
    x_ref, g_ref, w_ref = refs[:3]
    pos = 3
    wg_ref = None
    if has_gate:
        wg_ref = refs[pos]
        pos += 1
    n_seg = len(seg_tiles)
    out_refs = refs[pos:pos + n_seg]
    pos += n_seg
    gate_ref = None
    if has_gate:
        gate_ref = refs[pos]
        pos += 1
    h_ref = refs[pos]

    j = pl.program_id(1)

    @pl.when(j == 0)
    def _():
        h_ref[...] = _rmsnorm_rows(x_ref[...], g_ref[...]).astype(BF16)
        if has_gate:
            gate_ref[...] = _dot(h_ref[...], wg_ref[...])

    acc = _dot(h_ref[...], w_ref[...])
    lo = 0
    for out_ref, nt in zip(out_refs, seg_tiles):
        @pl.when((j >= lo) & (j < lo + nt))
        def _(out_ref=out_ref):
            out_ref[...] = acc.astype(out_ref.dtype)
        lo += nt


def norm_matmul(x, g, w, segs, *, w_gate=None, tm=1024, tn=1024):
    m, k = x.shape
    tm = _row_tile(m, tm)
    seg_tiles = tuple(n // tn for n, _ in segs)
    assert all(n % tn == 0 for n, _ in segs) and m % tm == 0
    starts = []
    lo = 0
    for nt in seg_tiles:
        starts.append(lo)
        lo += nt
    n_tiles = lo

    in_specs = [
        pl.BlockSpec((tm, k), lambda i, j: (i, 0)),
        pl.BlockSpec((1, k), lambda i, j: (0, 0)),
        pl.BlockSpec((k, tn), lambda i, j: (0, j)),
    ]
    args = [x, g.reshape(1, k), w]
    if w_gate is not None:
        in_specs.append(pl.BlockSpec((k, LANES), lambda i, j: (0, 0)))
        args.append(w_gate)

    out_shape, out_specs = [], []
    for (n, dt), nt, st in zip(segs, seg_tiles, starts):
        out_shape.append(jax.ShapeDtypeStruct((m, n), dt))
        out_specs.append(pl.BlockSpec(
            (tm, tn), lambda i, j, st=st, nt=nt: (i, jnp.clip(j - st, 0, nt - 1))))
    if w_gate is not None:
        out_shape.append(jax.ShapeDtypeStruct((m, LANES), F32))
        out_specs.append(pl.BlockSpec((tm, LANES), lambda i, j: (i, 0)))

    return pl.pallas_call(
        functools.partial(_norm_matmul_kernel, seg_tiles=seg_tiles, has_gate=w_gate is not None),
        grid=(m // tm, n_tiles),
        in_specs=in_specs,
        out_specs=out_specs,
        out_shape=out_shape,
        scratch_shapes=[pltpu.VMEM((tm, k), BF16)],
        compiler_params=_params("parallel", "arbitrary"),
        name="norm_matmul",
    )(*args)


def _proj_residual_kernel(a_ref, w_ref, y_ref, out_ref):
    out_ref[...] = y_ref[...] + _dot(a_ref[...], w_ref[...])


def proj_residual(a, w, y, *, tm=512):
    m, k = a.shape
    n = w.shape[1]
    tm = _row_tile(m, tm)
    return pl.pallas_call(
        _proj_residual_kernel,
        grid=(m // tm,),
        in_specs=[
            pl.BlockSpec((tm, k), lambda i: (i, 0)),
            pl.BlockSpec((k, n), lambda i: (0, 0)),
            pl.BlockSpec((tm, n), lambda i: (i, 0)),
        ],
        out_specs=pl.BlockSpec((tm, n), lambda i: (i, 0)),
        out_shape=jax.ShapeDtypeStruct((m, n), F32),
        compiler_params=_params("parallel"),
        name="proj_residual",
    )(a, w, y)


def _gate_act(pre):
    capped = GATE_CAP * jnp.tanh(pre / GATE_CAP)
    return capped, _log_sigmoid(capped)


def _head_out(hcur, gh, o):
    hn = hcur * lax.rsqrt(jnp.mean(hcur * hcur, axis=-1, keepdims=True) + EPS)
    return hn * gh * jax.nn.sigmoid(o)


def _mlstm_chunk_kernel(q_ref, k_ref, v_ref, o_ref, gc_ref, gr_ref, bc_ref, br_ref, gh_ref,
                        tri_ref, tril_ref, hs_ref, c_ref, n_ref, m_ref, *, heads, dk, dv, chunk):
    c_idx = pl.program_id(1)

    @pl.when(c_idx == 0)
    def _():
        c_ref[...] = jnp.zeros_like(c_ref)
        n_ref[...] = jnp.zeros_like(n_ref)
        m_ref[...] = jnp.zeros_like(m_ref)

    k_scale = dk ** -0.5
    tri = tri_ref[...]
    tril = tril_ref[...]
    li_c, lf_c = _gate_act(gc_ref[...] + br_ref[...])
    li_r, lf_r = _gate_act(gr_ref[...] + bc_ref[...])
    hi, lo = _split_bf16(lf_c)
    b_c = _dot(tril, hi) + _dot(tril, lo)
    hi, lo = _split_bf16(lf_r)
    b_r = _dot(hi, tri) + _dot(lo, tri)

    row = lax.broadcasted_iota(jnp.int32, (chunk, chunk), 0)
    col = lax.broadcasted_iota(jnp.int32, (chunk, chunk), 1)
    causal = col <= row

    for h in range(heads):
        q = q_ref[:, h * dk:(h + 1) * dk]
        k = k_ref[:, h * dk:(h + 1) * dk] * k_scale
        v = v_ref[:, h * dv:(h + 1) * dv]
        i_col = li_c[:, h:h + 1]
        b_col = b_c[:, heads + h:heads + h + 1]
        i_row = li_r[h:h + 1, :]
        b_row = b_r[heads + h:heads + h + 1, :]
        m_st = m_ref[0, h]
        c_st = c_ref[0, h]
        n_st = n_ref[0, h]

        log_d = jnp.where(causal, (b_col - b_row) + i_row, -jnp.inf)
        log_inter = b_col + m_st
        m_row = jnp.maximum(log_inter, jnp.max(log_d, axis=-1, keepdims=True))
        s = _dot_nt(q, k) * jnp.exp(log_d - m_row)
        w_inter = jnp.exp(log_inter - m_row)
        qf = q.astype(F32)
        num = w_inter * _dot(q, c_st.astype(BF16)) + _dot(s.astype(BF16), v)
        den = (w_inter * jnp.sum(qf * n_st, axis=-1, keepdims=True)
               + jnp.sum(s, axis=-1, keepdims=True))
        hcur = num / jnp.maximum(jnp.abs(den), jnp.exp(-m_row))
        gh = gh_ref[:, h * dv:(h + 1) * dv]
        o = o_ref[:, h * dv:(h + 1) * dv].astype(F32)
        hs_ref[:, h * dv:(h + 1) * dv] = _head_out(hcur, gh, o).astype(hs_ref.dtype)

        b_last = b_col[chunk - 1:chunk, :]
        log_w = (b_last - b_col) + i_col
        m_new = jnp.maximum(b_last + m_st, jnp.max(log_w, axis=0, keepdims=True))
        w = jnp.exp(log_w - m_new)
        decay = jnp.exp(b_last + m_st - m_new)
        wv = (w * v.astype(F32)).astype(BF16)
        c_ref[0, h] = decay * c_st + _dot_tn(k, wv)
        n_ref[0, h] = decay * n_st + jnp.sum(w * k.astype(F32), axis=0, keepdims=True)
        m_ref[0, h] = m_new


def mlstm_prompt(qkvo, gates, b_gate, g_head, batch, seq, heads, dk, dv):
    chunk = ML_CHUNK if seq % ML_CHUNK == 0 else seq
    n_chunks = seq // chunk
    hk, hv = heads * dk, heads * dv
    assert hk % LANES == 0 and hv % hk == 0
    kv_ratio = hv // hk
    gates_r = gates[:, :2 * heads].T
    tri = (jnp.arange(chunk)[:, None] <= jnp.arange(chunk)[None, :]).astype(BF16)

    def rows(b, c):
        return b * n_chunks + c

    hs, c_f, n_f, m_f = pl.pallas_call(
        functools.partial(_mlstm_chunk_kernel, heads=heads, dk=dk, dv=dv, chunk=chunk),
        grid=(batch, n_chunks),
        in_specs=[
            pl.BlockSpec((chunk, hk), lambda b, c: (rows(b, c), 0)),
            pl.BlockSpec((chunk, hk), lambda b, c: (rows(b, c), 1)),
            pl.BlockSpec((chunk, hv), lambda b, c: (rows(b, c), 2 // kv_ratio)),
            pl.BlockSpec((chunk, hv), lambda b, c: (rows(b, c), 2 // kv_ratio + 1)),
            pl.BlockSpec((chunk, LANES), lambda b, c: (rows(b, c), 0)),
            pl.BlockSpec((2 * heads, chunk), lambda b, c: (0, rows(b, c))),
            pl.BlockSpec((2 * heads, 1), lambda b, c: (0, 0)),
            pl.BlockSpec((1, LANES), lambda b, c: (0, 0)),
            pl.BlockSpec((1, hv), lambda b, c: (0, 0)),
            pl.BlockSpec((chunk, chunk), lambda b, c: (0, 0)),
            pl.BlockSpec((chunk, chunk), lambda b, c: (0, 0)),
        ],
        out_specs=[
            pl.BlockSpec((chunk, hv), lambda b, c: (rows(b, c), 0)),
            pl.BlockSpec((1, heads, dk, dv), lambda b, c: (b, 0, 0, 0)),
            pl.BlockSpec((1, heads, 1, dk), lambda b, c: (b, 0, 0, 0)),
            pl.BlockSpec((1, heads, 1, 1), lambda b, c: (b, 0, 0, 0)),
        ],
        out_shape=[
            jax.ShapeDtypeStruct((batch * seq, hv), BF16),
            jax.ShapeDtypeStruct((batch, heads, dk, dv), F32),
            jax.ShapeDtypeStruct((batch, heads, 1, dk), F32),
            jax.ShapeDtypeStruct((batch, heads, 1, 1), F32),
        ],
        compiler_params=_params("parallel", "arbitrary"),
        name="mlstm_chunk",
    )(qkvo, qkvo, qkvo, qkvo, gates, gates_r,
      b_gate.reshape(2 * heads, 1),
      jnp.pad(b_gate, (0, LANES - 2 * heads)).reshape(1, LANES),
      g_head.reshape(1, hv), tri, tri.T)
    return hs, c_f, n_f.reshape(batch, heads, dk), m_f.reshape(batch, heads)


def _mlstm_step_kernel(qkvo_ref, gate_ref, br_ref, gh_ref, c_ref, n_ref, m_ref,
                       hs_ref, c_out, n_out, m_out, *, heads, dk, dv):
    hk, hv = heads * dk, heads * dv
    k_scale = dk ** -0.5
    li, lf = _gate_act(gate_ref[0] + br_ref[...])
    eye = (lax.broadcasted_iota(jnp.int32, (dk, dk), 0)
           == lax.broadcasted_iota(jnp.int32, (dk, dk), 1))

    def to_col(x_row):
        return jnp.sum(jnp.where(eye, x_row, 0.0), axis=-1, keepdims=True)

    for h in range(heads):
        q = qkvo_ref[0, :, h * dk:(h + 1) * dk].astype(F32)
        k = qkvo_ref[0, :, hk + h * dk:hk + (h + 1) * dk].astype(F32) * k_scale
        v = qkvo_ref[0, :, 2 * hk + h * dv:2 * hk + (h + 1) * dv].astype(F32)
        o = qkvo_ref[0, :, 2 * hk + hv + h * dv:2 * hk + hv + (h + 1) * dv].astype(F32)
        log_i = li[:, h:h + 1]
        log_f = lf[:, heads + h:heads + h + 1]
        c_st = c_ref[0, h]
        n_st = n_ref[0, h]
        m_st = m_ref[0, h]

        log_inter = log_f + m_st
        m_row = jnp.maximum(log_inter, log_i)
        s = jnp.sum(q * k, axis=-1, keepdims=True) * jnp.exp(log_i - m_row)
        w_inter = jnp.exp(log_inter - m_row)
        q_c = jnp.sum(to_col(q) * c_st, axis=0, keepdims=True)
        num = w_inter * q_c + s * v
        den = w_inter * jnp.sum(q * n_st, axis=-1, keepdims=True) + s
        hcur = num / jnp.maximum(jnp.abs(den), jnp.exp(-m_row))
        gh = gh_ref[:, h * dv:(h + 1) * dv]
        hs_ref[0, :, h * dv:(h + 1) * dv] = _head_out(hcur, gh, o).astype(hs_ref.dtype)

        m_new = jnp.maximum(log_f + m_st, log_i)
        w = jnp.exp(log_i - m_new)
        decay = jnp.exp(log_f + m_st - m_new)
        c_out[0, h] = decay * c_st + (w * to_col(k)) * v
        n_out[0, h] = decay * n_st + w * k
        m_out[0, h] = m_new


def mlstm_sample(qkvo, gates, b_gate, g_head, c0, n0, m0):
    batch, heads, dk, dv = c0.shape
    hk, hv = heads * dk, heads * dv
    width = qkvo.shape[1]
    hs, c_f, n_f, m_f = pl.pallas_call(
        functools.partial(_mlstm_step_kernel, heads=heads, dk=dk, dv=dv),
        grid=(batch,),
        in_specs=[
            pl.BlockSpec((1, 1, width), lambda b: (b, 0, 0)),
            pl.BlockSpec((1, 1, LANES), lambda b: (b, 0, 0)),
            pl.BlockSpec((1, LANES), lambda b: (0, 0)),
            pl.BlockSpec((1, hv), lambda b: (0, 0)),
            pl.BlockSpec((1, heads, dk, dv), lambda b: (b, 0, 0, 0)),
            pl.BlockSpec((1, heads, 1, dk), lambda b: (b, 0, 0, 0)),
            pl.BlockSpec((1, heads, 1, 1), lambda b: (b, 0, 0, 0)),
        ],
        out_specs=[
            pl.BlockSpec((1, 1, hv), lambda b: (b, 0, 0)),
            pl.BlockSpec((1, heads, dk, dv), lambda b: (b, 0, 0, 0)),
            pl.BlockSpec((1, heads, 1, dk), lambda b: (b, 0, 0, 0)),
            pl.BlockSpec((1, heads, 1, 1), lambda b: (b, 0, 0, 0)),
        ],
        out_shape=[
            jax.ShapeDtypeStruct((batch, 1, hv), BF16),
            jax.ShapeDtypeStruct((batch, heads, dk, dv), F32),
            jax.ShapeDtypeStruct((batch, heads, 1, dk), F32),
            jax.ShapeDtypeStruct((batch, heads, 1, 1), F32),
        ],
        compiler_params=_params("parallel"),
        name="mlstm_step",
    )(qkvo.reshape(batch, 1, width), gates.reshape(batch, 1, LANES),
      jnp.pad(b_gate, (0, LANES - 2 * heads)).reshape(1, LANES), g_head.reshape(1, hv),
      c0, n0.reshape(batch, heads, 1, dk), m0.reshape(batch, heads, 1, 1))
    return hs.reshape(batch, hv), c_f, n_f.reshape(batch, heads, dk), m_f.reshape(batch, heads)


def _suffix_matrix(tk):
    j = jnp.arange(tk)[:, None]
    s = jnp.arange(tk)[None, :]
    half = jnp.concatenate([(j >= s).astype(BF16), jnp.ones((tk, tk), BF16)], axis=1)
    return jnp.concatenate([half, half], axis=0)


def _sb_block(z, mask, suffix, carry, tk):
    n = z.shape[1] // tk
    neg = -z
    log_om = jnp.minimum(neg, 0.0) - jnp.log(1.0 + jnp.exp(jnp.minimum(z, neg)))
    args = [None] * n
    for c in range(n - 1, -1, -1):
        cols = slice(c * tk, (c + 1) * tk)
        part = log_om[:, cols]
        if mask is not None and c == n - 1:
            part = jnp.where(mask, part, 0.0)
        hi, lo = _split_bf16(part)
        r = _dot(jnp.concatenate([hi, lo], axis=1), suffix)
        args[c] = (z[:, cols] + carry) + r[:, :tk]
        if mask is not None and c == n - 1:
            args[c] = jnp.where(mask, args[c], -jnp.inf)
        carry = carry + r[:, tk:]
    return jnp.exp(jnp.concatenate(args, axis=1)), carry


def _sb_prompt_kernel(bias_ref, q_ref, k_ref, v_ref, suffix_ref, o_ref, kb_ref, vb_ref,
                      *, tq, tk, nq, scale):
    h = pl.program_id(1)
    i = pl.program_id(2)
    n_sub = tq // tk

    hd = q_ref.shape[1]

    @pl.when(i == 0)
    def _():
        kb_ref[:, :hd] = (k_ref[...] * scale).astype(BF16)
        rows = kb_ref.shape[0]
        b0 = jnp.full((rows, hd), bias_ref[h], F32)
        p0 = b0.astype(BF16).astype(F32)
        b1 = b0 - p0
        p1 = b1.astype(BF16).astype(F32)
        p2 = b1 - p1
        lane = lax.broadcasted_iota(jnp.int32, (rows, hd), 1)
        pieces = jnp.where(lane == 0, p0, jnp.where(lane == 1, p1, jnp.where(lane == 2, p2, 0.0)))
        kb_ref[:, hd:] = pieces.astype(BF16)
        vb_ref[...] = v_ref[...].astype(BF16)

    suffix = suffix_ref[...]
    strict = (lax.broadcasted_iota(jnp.int32, (tk, tk), 1)
              < lax.broadcasted_iota(jnp.int32, (tk, tk), 0))
    q_aug = jnp.concatenate([q_ref[...], jnp.ones((tq, hd), BF16)], axis=1)

    def logits(start):
        return _dot_nt(q_aug, kb_ref[start:start + tq, :])

    def own_keys(z, start):
        weights, carries = [], []
        for r in range(n_sub):
            a, carry = _sb_block(z[r * tk:(r + 1) * tk, :(r + 1) * tk], strict, suffix,
                                 jnp.zeros((tk, tk), F32), tk)
            if r < n_sub - 1:
                a = jnp.concatenate([a, jnp.zeros((tk, (n_sub - 1 - r) * tk), F32)], axis=1)
            weights.append(a)
            carries.append(carry)
        a = jnp.concatenate(weights, axis=0).astype(BF16)
        return jnp.concatenate(carries, axis=0), _dot(a, vb_ref[start:start + tq, :])

    def older_keys(z, start, carry, acc):
        a, carry = _sb_block(z, None, suffix, carry, tk)
        return carry, acc + _dot(a.astype(BF16), vb_ref[start:start + tq, :])

    for blk in range(nq):
        @pl.when(i == blk)
        def _(blk=blk):
            starts = [(blk - b) * tq for b in range(blk + 1)]
            z_next = logits(starts[0])
            carry = acc = None
            for b, start in enumerate(starts):
                z = z_next
                if b + 1 < len(starts):
                    z_next = logits(starts[b + 1])
                if b == 0:
                    carry, acc = own_keys(z, start)
                else:
                    carry, acc = older_keys(z, start, carry, acc)
            o_ref[...] = acc.astype(o_ref.dtype)


def sb_prompt_attention(q, k, v, logit_bias, batch, seq, heads, hd):
    tk = SB_TK
    tq = SB_TQ if seq % SB_TQ == 0 else tk
    assert tq % tk == 0 and seq % tq == 0 and hd == LANES
    nq = seq // tq
    return pl.pallas_call(
        functools.partial(_sb_prompt_kernel, tq=tq, tk=tk, nq=nq, scale=hd ** -0.5),
        grid=(batch, heads, nq),
        in_specs=[
            pl.BlockSpec(memory_space=pltpu.SMEM),
            pl.BlockSpec((tq, hd), lambda b, h, i: (b * nq + i, h)),
            pl.BlockSpec((seq, hd), lambda b, h, i: (b, h)),
            pl.BlockSpec((seq, hd), lambda b, h, i: (b, h)),
            pl.BlockSpec((2 * tk, 2 * tk), lambda b, h, i: (0, 0)),
        ],
        out_specs=pl.BlockSpec((tq, hd), lambda b, h, i: (b * nq + i, h)),
        out_shape=jax.ShapeDtypeStruct((batch * seq, heads * hd), BF16),
        scratch_shapes=[pltpu.VMEM((seq, 2 * hd), BF16), pltpu.VMEM((seq, hd), BF16)],
        compiler_params=_params("parallel", "parallel", "arbitrary"),
        name="sb_prompt",
    )(logit_bias, q, k, v, _suffix_matrix(tk))


def _decode_phases(in_refs, o_ref, scratch_refs, *, pages, heads, hd, page, past_len, scale):
    q_ref, kn_ref, vn_ref, bias_ref, suffix_ref = in_refs[:5]
    k_refs = in_refs[5:5 + pages]
    v_refs = in_refs[5 + pages:5 + 2 * pages]
    qbd_ref, carry_ref, acc_ref = scratch_refs
    width = heads * hd

    def own_columns():
        head_of_col = lax.broadcasted_iota(jnp.int32, (heads, width), 1) // hd
        return head_of_col == lax.broadcasted_iota(jnp.int32, (heads, width), 0)

    def init():
        own = own_columns()
        bias = bias_ref[...]
        q_bd = jnp.where(own, q_ref[0].astype(F32), 0.0)
        qbd_ref[...] = q_bd.astype(BF16)
        k_new = kn_ref[0]
        v_new = vn_ref[0]
        z_new = jnp.sum(q_bd.astype(F32) * k_new, axis=-1, keepdims=True) * scale + bias
        visible = jnp.full((heads, 1), past_len, jnp.int32) < past_len
        sp = _softplus(z_new)
        log_om = jnp.where(visible, -sp, 0.0)
        a_new = jnp.where(visible, jnp.exp(z_new - sp), 0.0)
        carry_ref[...] = jnp.broadcast_to(log_om, carry_ref.shape)
        acc_ref[...] = a_new * v_new

    def token_rows(refs):
        blocks = []
        for ref in reversed(refs):
            x = jnp.swapaxes(ref[0].reshape(page, heads, hd).astype(BF16), 0, 1)
            blocks.append(jnp.concatenate([x[hh] for hh in range(heads)], axis=1))
        return jnp.concatenate(blocks, axis=0)

    def body():
        keys = token_rows(k_refs)
        values = token_rows(v_refs)
        z = _dot_nt(qbd_ref[...], keys) * scale + bias_ref[...]
        a, carry = _sb_block(z, None, suffix_ref[...], carry_ref[...], page)
        carry_ref[...] = carry
        acc_ref[...] += _dot(a.astype(BF16), values)

    def finish():
        acc = jnp.where(own_columns(), acc_ref[...], 0.0)
        o_ref[0] = jnp.sum(acc, axis=0, keepdims=True).astype(o_ref.dtype)

    return init, body, finish


def _decode_call_parts(q, k_new, v_new, k_pool, v_pool, first_page, page_table, logit_bias,
                       heads, hd, steps):
    batch, width = q.shape
    page = k_pool.shape[1] // heads
    n_pages = page_table.shape[1]
    assert page == LANES and hd == LANES and n_pages % steps == 0
    pages = n_pages // steps

    def page_spec(idx):
        return pl.BlockSpec(
            (1, page * heads, hd),
            lambda b, p, pt, idx=idx: (first_page + pt[b, n_pages - 1 - (p * pages + idx)], 0, 0))

    vec = pl.BlockSpec((1, 1, width), lambda b, p, pt: (b, 0, 0))
    static = dict(pages=pages, heads=heads, hd=hd, page=page, past_len=n_pages * page,
                  scale=hd ** -0.5)
    in_specs = ([vec, vec, vec,
                 pl.BlockSpec((heads, 1), lambda b, p, pt: (0, 0)),
                 pl.BlockSpec((2 * page, 2 * page), lambda b, p, pt: (0, 0))]
                + [page_spec(i) for i in range(pages)] * 2)
    args = [q.reshape(batch, 1, width), k_new.reshape(batch, 1, width),
            v_new.reshape(batch, 1, width), logit_bias.reshape(heads, 1), _suffix_matrix(page),
            *([k_pool] * pages), *([v_pool] * pages)]
    scratch = [pltpu.VMEM((heads, width), BF16), pltpu.VMEM((heads, page), F32),
               pltpu.VMEM((heads, width), F32)]
    return static, in_specs, args, vec, jax.ShapeDtypeStruct((batch, 1, width), BF16), scratch


def _sb_decode_kernel(pt_ref, *refs, n_in, **static):
    p = pl.program_id(1)
    init, body, finish = _decode_phases(refs[:n_in], refs[n_in], refs[n_in + 1:], **static)
    pl.when(p == 0)(init)
    body()
    pl.when(p == pl.num_programs(1) - 1)(finish)


def sb_decode_attention(q, k_new, v_new, k_pool, v_pool, first_page, page_table, logit_bias,
                        heads, hd):
    n_pages = page_table.shape[1]
    pages = DEC_PAGES_PER_STEP if n_pages % DEC_PAGES_PER_STEP == 0 else 1
    static, in_specs, args, out_spec, out_shape, scratch = _decode_call_parts(
        q, k_new, v_new, k_pool, v_pool, first_page, page_table, logit_bias, heads, hd,
        n_pages // pages)
    out = pl.pallas_call(
        functools.partial(_sb_decode_kernel, n_in=len(in_specs), **static),
        grid_spec=pltpu.PrefetchScalarGridSpec(
            num_scalar_prefetch=1, grid=(q.shape[0], n_pages // pages),
            in_specs=in_specs, out_specs=out_spec, scratch_shapes=scratch),
        out_shape=out_shape,
        compiler_params=_params("parallel", "arbitrary"),
        name="sb_decode",
    )(page_table, *args)
    return out.reshape(q.shape)


def _mlp_phases(y_ref, g_ref, wu_ref, wd_ref, gf_ref, out_ref, h_ref):
    def init():
        y = y_ref[...]
        h_ref[...] = _rmsnorm_rows(y, g_ref[...]).astype(BF16)
        out_ref[...] = y

    def body():
        u = jnp.maximum(_dot(h_ref[...], wu_ref[...]), 0.0)
        out_ref[...] += _dot((u * u).astype(BF16), wd_ref[...])

    def finish():
        if gf_ref is not None:
            out_ref[...] = _rmsnorm_rows(out_ref[...], gf_ref[...])

    return init, body, finish


def _mlp_kernel(*refs, final_norm, n_dec_in, dec_static):
    if n_dec_in:
        refs = refs[1:]
    n_mlp_in = 5 if final_norm else 4
    mlp_in = list(refs[:n_mlp_in]) + ([] if final_norm else [None])
    dec_in = refs[n_mlp_in:n_mlp_in + n_dec_in]
    pos = n_mlp_in + n_dec_in
    out_ref = refs[pos]
    pos += 1
    phases = []
    if n_dec_in:
        phases.append(_decode_phases(dec_in, refs[pos], refs[pos + 2:], **dec_static))
        pos += 1
    phases.insert(0, _mlp_phases(*mlp_in, out_ref, refs[pos]))

    def run_all(which):
        def run():
            for phase in phases:
                phase[which]()
        return run

    f = pl.program_id(1)
    pl.when(f == 0)(run_all(0))
    run_all(1)()
    pl.when(f == pl.num_programs(1) - 1)(run_all(2))


def mlp_residual(y, g, w_up, w_down, final_g=None, decode=None, *, tm=MLP_TM, tf=MLP_TF):
    m, d = y.shape
    d_ff = w_up.shape[1]
    tm = _row_tile(m, tm)
    assert d_ff % tf == 0
    grid = (m // tm, d_ff // tf)
    in_specs = [
        pl.BlockSpec((tm, d), lambda i, f, *_: (i, 0),
                     **({"pipeline_mode": pl.Buffered(1)} if decode else {})),
        pl.BlockSpec((1, d), lambda i, f, *_: (0, 0)),
        pl.BlockSpec((d, tf), lambda i, f, *_: (0, f)),
        pl.BlockSpec((tf, d), lambda i, f, *_: (f, 0)),
    ]
    args = [y, g.reshape(1, d), w_up, w_down]
    if final_g is not None:
        in_specs.append(pl.BlockSpec((1, d), lambda i, f, *_: (0, 0)))
        args.append(final_g.reshape(1, d))
    out_specs = [pl.BlockSpec((tm, d), lambda i, f, *_: (i, 0))]
    out_shape = [jax.ShapeDtypeStruct((m, d), F32)]
    scratch = [pltpu.VMEM((tm, d), BF16)]
    dec_static, n_dec_in, prefetch = None, 0, []
    if decode:
        assert decode["q"].shape[0] == grid[0]
        dec_static, dec_specs, dec_args, o_spec, o_shape, dec_scratch = _decode_call_parts(
            steps=grid[1], **decode)
        n_dec_in = len(dec_specs)
        in_specs += dec_specs
        args += dec_args
        out_specs.append(o_spec)
        out_shape.append(o_shape)
        scratch += dec_scratch
        prefetch = [decode["page_table"]]
    outs = pl.pallas_call(
        functools.partial(_mlp_kernel, final_norm=final_g is not None, n_dec_in=n_dec_in,
                          dec_static=dec_static),
        grid_spec=pltpu.PrefetchScalarGridSpec(
            num_scalar_prefetch=len(prefetch), grid=grid, in_specs=in_specs,
            out_specs=out_specs, scratch_shapes=scratch),
        out_shape=out_shape,
        compiler_params=_params("parallel", "arbitrary"),
        name="mlp_residual",
    )(*prefetch, *args)
    if decode:
        return outs[0], outs[1].reshape(decode["q"].shape)
    return outs[0]


def kernel(x_prompt, x_sample, state_C, state_n, state_m, cache_k, cache_v, page_table,
           norm_mix_g, norm_ffn_g, ml_w_in, ml_b_gate, ml_head_g, ml_w_out,
           sb_w_in, sb_logit_bias, sb_w_out, ffn_w_up, ffn_w_down, final_g):
    batch, seq, d = x_prompt.shape
    dec_batch, dec_seq, _ = x_sample.shape
    assert dec_seq == 1
    assert norm_mix_g.shape[0] == 2
    ml_heads, ml_dk, ml_dv = state_C.shape[2:]
    sb_heads, sb_hd = cache_k.shape[3:]
    hk, hv = ml_heads * ml_dk, ml_heads * ml_dv
    ml_main = 2 * hk + 2 * hv
    m = batch * seq

    ml_in = ml_w_in[0].astype(BF16)
    ml_gate = jnp.pad(ml_in[:, ml_main:], ((0, 0), (0, LANES - 2 * ml_heads)))
    ml_out = ml_w_out[0].astype(BF16)
    sb_in = sb_w_in[0].astype(BF16)
    sb_out = sb_w_out[0].astype(BF16)
    w_up = ffn_w_up.astype(BF16)
    w_down = ffn_w_down.astype(BF16)
    ml_segs = [(ml_main, BF16)]
    sb_segs = [(d, BF16), (d, F32), (d, F32)]

    ys = x_sample.reshape(dec_batch, d)
    qkvo, gates = norm_matmul(ys, norm_mix_g[0], ml_in, ml_segs, w_gate=ml_gate)
    hs, c_s, n_s, m_s = mlstm_sample(qkvo, gates, ml_b_gate[0], ml_head_g[0],
                                     state_C[0], state_n[0], state_m[0])
    ys = proj_residual(hs, ml_out, ys)
    ys = mlp_residual(ys, norm_ffn_g[0], w_up[0], w_down[0])
    q_s, k_s, v_s = norm_matmul(ys, norm_mix_g[1], sb_in, sb_segs)

    n_layers, n_pool, page = cache_k.shape[:3]
    pool_shape = (n_layers * n_pool, page * sb_heads, sb_hd)
    pool = dict(k_pool=cache_k.reshape(pool_shape), v_pool=cache_v.reshape(pool_shape),
                first_page=0, logit_bias=sb_logit_bias[0], heads=sb_heads, hd=sb_hd)
    n_hosts = 2
    per_host = dec_batch // n_hosts
    ride = (dec_batch % n_hosts == 0 and per_host == m // _row_tile(m, MLP_TM)
            and page_table.shape[1] % (w_up.shape[2] // MLP_TF) == 0)

    def decode_share(r):
        rows = slice(r * per_host, (r + 1) * per_host)
        return dict(q=q_s[rows], k_new=k_s[rows], v_new=v_s[rows], page_table=page_table[rows],
                    **pool)

    yp = x_prompt.reshape(m, d)
    qkvo, gates = norm_matmul(yp, norm_mix_g[0], ml_in, ml_segs, w_gate=ml_gate)
    hs, c_p, n_p, m_p = mlstm_prompt(qkvo, gates, ml_b_gate[0], ml_head_g[0],
                                     batch, seq, ml_heads, ml_dk, ml_dv)
    yp = proj_residual(hs, ml_out, yp)
    if ride:
        yp, o_first = mlp_residual(yp, norm_ffn_g[0], w_up[0], w_down[0], decode=decode_share(0))
    else:
        yp = mlp_residual(yp, norm_ffn_g[0], w_up[0], w_down[0])

    q_p, k_p, v_p = norm_matmul(yp, norm_mix_g[1], sb_in, sb_segs)
    o = sb_prompt_attention(q_p, k_p, v_p, sb_logit_bias[0], batch, seq, sb_heads, sb_hd)
    yp = proj_residual(o, sb_out, yp)
    if ride:
        yp, o_second = mlp_residual(yp, norm_ffn_g[1], w_up[1], w_down[1], final_g,
                                    decode=decode_share(1))
        o_s = jnp.concatenate([o_first, o_second], axis=0)
    else:
        yp = mlp_residual(yp, norm_ffn_g[1], w_up[1], w_down[1], final_g)
        o_s = sb_decode_attention(q_s, k_s, v_s, page_table=page_table, **pool)

    ys = proj_residual(o_s, sb_out, ys)
    ys = mlp_residual(ys, norm_ffn_g[1], w_up[1], w_down[1], final_g)

    kv_p = (1, batch, seq, sb_heads, sb_hd)
    kv_s = (1, dec_batch, dec_seq, sb_heads, sb_hd)
    return (yp.reshape(batch, seq, d), ys.reshape(dec_batch, dec_seq, d),
            c_p[None], n_p[None], m_p[None], k_p.reshape(kv_p), v_p.reshape(kv_p),
            c_s[None], n_s[None], m_s[None], k_s.reshape(kv_s), v_s.reshape(kv_s))
```

```python
import functools

import jax
import jax.numpy as jnp
from jax import lax
from jax.experimental import pallas as pl
from jax.experimental.pallas import tpu as pltpu

F32 = jnp.float32
BF16 = jnp.bfloat16

EPS = 1e-6
GATE_CAP = 15.0
LANES = 128
VMEM_LIMIT_BYTES = 56 * 1024 * 1024

ML_CHUNK = 128
SB_TQ = 512
SB_TK = 128
SB_HEAD_GROUP = 2
DEC_PAGES_PER_STEP = 4
MLP_TM = 1024
MLP_TF = 512


def _params(*sem):
    return pltpu.CompilerParams(dimension_semantics=sem, vmem_limit_bytes=VMEM_LIMIT_BYTES)


def _row_tile(m, want):
    return want if m % want == 0 else m


def _rmsnorm_rows(x, g):
    return x * lax.rsqrt(jnp.mean(x * x, axis=-1, keepdims=True) + EPS) * g


def _log_sigmoid(x):
    return jnp.minimum(x, 0.0) - jnp.log1p(jnp.exp(-jnp.abs(x)))


def _softplus(x):
    return jnp.maximum(x, 0.0) + jnp.log1p(jnp.exp(-jnp.abs(x)))


def _split_bf16(x):
    hi = x.astype(BF16)
    lo = (x - hi.astype(F32)).astype(BF16)
    return hi, lo


def _dot(a, b):
    return jnp.dot(a, b, preferred_element_type=F32)


def _dot_nt(a, b):
    return lax.dot_general(a, b, (((1,), (1,)), ((), ())), preferred_element_type=F32)


def _dot_tn(a, b):
    return lax.dot_general(a, b, (((0,), (0,)), ((), ())), preferred_element_type=F32)


def _norm_matmul_kernel(*refs, seg_tiles, has_gate):
    x_ref, g_ref, w_ref = refs[:3]
    pos = 3
    wg_ref = None
    if has_gate:
        wg_ref = refs[pos]
        pos += 1
    n_seg = len(seg_tiles)
    out_refs = refs[pos:pos + n_seg]
    pos += n_seg
    gate_ref = None
    if has_gate:
        gate_ref = refs[pos]
        pos += 1
    h_ref = refs[pos]

    j = pl.program_id(1)

    @pl.when(j == 0)
    def _():
        h_ref[...] = _rmsnorm_rows(x_ref[...], g_ref[...]).astype(BF16)
        if has_gate:
            gate_ref[...] = _dot(h_ref[...], wg_ref[...])

    acc = _dot(h_ref[...], w_ref[...])
    lo = 0
    for out_ref, nt in zip(out_refs, seg_tiles):
        @pl.when((j >= lo) & (j < lo + nt))
        def _(out_ref=out_ref):
            out_ref[...] = acc.astype(out_ref.dtype)
        lo += nt


def norm_matmul(x, g, w, segs, *, w_gate=None, tm=1024, tn=1024):
    m, k = x.shape
    tm = _row_tile(m, tm)
    seg_tiles = tuple(n // tn for n, _ in segs)
    assert all(n % tn == 0 for n, _ in segs) and m % tm == 0
    starts = []
    lo = 0
    for nt in seg_tiles:
        starts.append(lo)
        lo += nt
    n_tiles = lo

    in_specs = [
        pl.BlockSpec((tm, k), lambda i, j: (i, 0)),
        pl.BlockSpec((1, k), lambda i, j: (0, 0)),
        pl.BlockSpec((k, tn), lambda i, j: (0, j)),
    ]
    args = [x, g.reshape(1, k), w]
    if w_gate is not None:
        in_specs.append(pl.BlockSpec((k, LANES), lambda i, j: (0, 0)))
        args.append(w_gate)

    out_shape, out_specs = [], []
    for (n, dt), nt, st in zip(segs, seg_tiles, starts):
        out_shape.append(jax.ShapeDtypeStruct((m, n), dt))
        out_specs.append(pl.BlockSpec(
            (tm, tn), lambda i, j, st=st, nt=nt: (i, jnp.clip(j - st, 0, nt - 1))))
    if w_gate is not None:
        out_shape.append(jax.ShapeDtypeStruct((m, LANES), F32))
        out_specs.append(pl.BlockSpec((tm, LANES), lambda i, j: (i, 0)))

    return pl.pallas_call(
        functools.partial(_norm_matmul_kernel, seg_tiles=seg_tiles, has_gate=w_gate is not None),
        grid=(m // tm, n_tiles),
        in_specs=in_specs,
        out_specs=out_specs,
        out_shape=out_shape,
        scratch_shapes=[pltpu.VMEM((tm, k), BF16)],
        compiler_params=_params("parallel", "arbitrary"),
        name="norm_matmul",
    )(*args)


def _proj_residual_kernel(a_ref, w_ref, y_ref, out_ref):
    out_ref[...] = y_ref[...] + _dot(a_ref[...], w_ref[...])


def proj_residual(a, w, y, *, tm=512):
    m, k = a.shape
    n = w.shape[1]
    tm = _row_tile(m, tm)
    return pl.pallas_call(
        _proj_residual_kernel,
        grid=(m // tm,),
        in_specs=[
            pl.BlockSpec((tm, k), lambda i: (i, 0)),
            pl.BlockSpec((k, n), lambda i: (0, 0)),
            pl.BlockSpec((tm, n), lambda i: (i, 0)),
        ],
        out_specs=pl.BlockSpec((tm, n), lambda i: (i, 0)),
        out_shape=jax.ShapeDtypeStruct((m, n), F32),
        compiler_params=_params("parallel"),
        name="proj_residual",
    )(a, w, y)


def _gate_act(pre):
    capped = GATE_CAP * jnp.tanh(pre / GATE_CAP)
    return capped, _log_sigmoid(capped)


def _head_out(hcur, gh, o):
    hn = hcur * lax.rsqrt(jnp.mean(hcur * hcur, axis=-1, keepdims=True) + EPS)
    return hn * gh * jax.nn.sigmoid(o)


def _mlstm_chunk_kernel(q_ref, k_ref, v_ref, o_ref, gc_ref, gr_ref, bc_ref, br_ref, gh_ref,
                        tri_ref, tril_ref, hs_ref, c_ref, n_ref, m_ref, *, heads, dk, dv, chunk):
    c_idx = pl.program_id(1)

    @pl.when(c_idx == 0)
    def _():
        c_ref[...] = jnp.zeros_like(c_ref)
        n_ref[...] = jnp.zeros_like(n_ref)
        m_ref[...] = jnp.zeros_like(m_ref)

    k_scale = dk ** -0.5
    tri = tri_ref[...]
    tril = tril_ref[...]
    li_c, lf_c = _gate_act(gc_ref[...] + br_ref[...])
    li_r, lf_r = _gate_act(gr_ref[...] + bc_ref[...])
    hi, lo = _split_bf16(lf_c)
    b_c = _dot(tril, hi) + _dot(tril, lo)
    hi, lo = _split_bf16(lf_r)
    b_r = _dot(hi, tri) + _dot(lo, tri)

    row = lax.broadcasted_iota(jnp.int32, (chunk, chunk), 0)
    col = lax.broadcasted_iota(jnp.int32, (chunk, chunk), 1)
    causal = col <= row

    for h in range(heads):
        q = q_ref[:, h * dk:(h + 1) * dk]
        k = k_ref[:, h * dk:(h + 1) * dk] * k_scale
        v = v_ref[:, h * dv:(h + 1) * dv]
        i_col = li_c[:, h:h + 1]
        b_col = b_c[:, heads + h:heads + h + 1]
        i_row = li_r[h:h + 1, :]
        b_row = b_r[heads + h:heads + h + 1, :]
        m_st = m_ref[0, h]
        c_st = c_ref[0, h]
        n_st = n_ref[0, h]

        log_d = jnp.where(causal, (b_col - b_row) + i_row, -jnp.inf)
        log_inter = b_col + m_st
        m_row = jnp.maximum(log_inter, jnp.max(log_d, axis=-1, keepdims=True))
        s = _dot_nt(q, k) * jnp.exp(log_d - m_row)
        w_inter = jnp.exp(log_inter - m_row)
        qf = q.astype(F32)
        num = w_inter * _dot(q, c_st.astype(BF16)) + _dot(s.astype(BF16), v)
        den = (w_inter * jnp.sum(qf * n_st, axis=-1, keepdims=True)
               + jnp.sum(s, axis=-1, keepdims=True))
        hcur = num / jnp.maximum(jnp.abs(den), jnp.exp(-m_row))
        gh = gh_ref[:, h * dv:(h + 1) * dv]
        o = o_ref[:, h * dv:(h + 1) * dv].astype(F32)
        hs_ref[:, h * dv:(h + 1) * dv] = _head_out(hcur, gh, o).astype(hs_ref.dtype)

        b_last = b_col[chunk - 1:chunk, :]
        log_w = (b_last - b_col) + i_col
        m_new = jnp.maximum(b_last + m_st, jnp.max(log_w, axis=0, keepdims=True))
        w = jnp.exp(log_w - m_new)
        decay = jnp.exp(b_last + m_st - m_new)
        wv = (w * v.astype(F32)).astype(BF16)
        c_ref[0, h] = decay * c_st + _dot_tn(k, wv)
        n_ref[0, h] = decay * n_st + jnp.sum(w * k.astype(F32), axis=0, keepdims=True)
        m_ref[0, h] = m_new


def mlstm_prompt(qkvo, gates, b_gate, g_head, batch, seq, heads, dk, dv):
    chunk = ML_CHUNK if seq % ML_CHUNK == 0 else seq
    n_chunks = seq // chunk
    hk, hv = heads * dk, heads * dv
    assert hk % LANES == 0 and hv % hk == 0
    kv_ratio = hv // hk
    gates_r = gates[:, :2 * heads].T
    tri = (jnp.arange(chunk)[:, None] <= jnp.arange(chunk)[None, :]).astype(BF16)

    def rows(b, c):
        return b * n_chunks + c

    hs, c_f, n_f, m_f = pl.pallas_call(
        functools.partial(_mlstm_chunk_kernel, heads=heads, dk=dk, dv=dv, chunk=chunk),
        grid=(batch, n_chunks),
        in_specs=[
            pl.BlockSpec((chunk, hk), lambda b, c: (rows(b, c), 0)),
            pl.BlockSpec((chunk, hk), lambda b, c: (rows(b, c), 1)),
            pl.BlockSpec((chunk, hv), lambda b, c: (rows(b, c), 2 // kv_ratio)),
            pl.BlockSpec((chunk, hv), lambda b, c: (rows(b, c), 2 // kv_ratio + 1)),
            pl.BlockSpec((chunk, LANES), lambda b, c: (rows(b, c), 0)),
            pl.BlockSpec((2 * heads, chunk), lambda b, c: (0, rows(b, c))),
            pl.BlockSpec((2 * heads, 1), lambda b, c: (0, 0)),
            pl.BlockSpec((1, LANES), lambda b, c: (0, 0)),
            pl.BlockSpec((1, hv), lambda b, c: (0, 0)),
            pl.BlockSpec((chunk, chunk), lambda b, c: (0, 0)),
            pl.BlockSpec((chunk, chunk), lambda b, c: (0, 0)),
        ],
        out_specs=[
            pl.BlockSpec((chunk, hv), lambda b, c: (rows(b, c), 0)),
            pl.BlockSpec((1, heads, dk, dv), lambda b, c: (b, 0, 0, 0)),
            pl.BlockSpec((1, heads, 1, dk), lambda b, c: (b, 0, 0, 0)),
            pl.BlockSpec((1, heads, 1, 1), lambda b, c: (b, 0, 0, 0)),
        ],
        out_shape=[
            jax.ShapeDtypeStruct((batch * seq, hv), BF16),
            jax.ShapeDtypeStruct((batch, heads, dk, dv), F32),
            jax.ShapeDtypeStruct((batch, heads, 1, dk), F32),
            jax.ShapeDtypeStruct((batch, heads, 1, 1), F32),
        ],
        compiler_params=_params("parallel", "arbitrary"),
        name="mlstm_chunk",
    )(qkvo, qkvo, qkvo, qkvo, gates, gates_r,
      b_gate.reshape(2 * heads, 1),
      jnp.pad(b_gate, (0, LANES - 2 * heads)).reshape(1, LANES),
      g_head.reshape(1, hv), tri, tri.T)
    return hs, c_f, n_f.reshape(batch, heads, dk), m_f.reshape(batch, heads)


def _mlstm_step_kernel(qkvo_ref, gate_ref, br_ref, gh_ref, c_ref, n_ref, m_ref,
                       hs_ref, c_out, n_out, m_out, *, heads, dk, dv):
    hk, hv = heads * dk, heads * dv
    k_scale = dk ** -0.5
    li, lf = _gate_act(gate_ref[0] + br_ref[...])
    eye = (lax.broadcasted_iota(jnp.int32, (dk, dk), 0)
           == lax.broadcasted_iota(jnp.int32, (dk, dk), 1))

    def to_col(x_row):
        return jnp.sum(jnp.where(eye, x_row, 0.0), axis=-1, keepdims=True)

    for h in range(heads):
        q = qkvo_ref[0, :, h * dk:(h + 1) * dk].astype(F32)
        k = qkvo_ref[0, :, hk + h * dk:hk + (h + 1) * dk].astype(F32) * k_scale
        v = qkvo_ref[0, :, 2 * hk + h * dv:2 * hk + (h + 1) * dv].astype(F32)
        o = qkvo_ref[0, :, 2 * hk + hv + h * dv:2 * hk + hv + (h + 1) * dv].astype(F32)
        log_i = li[:, h:h + 1]
        log_f = lf[:, heads + h:heads + h + 1]
        c_st = c_ref[0, h]
        n_st = n_ref[0, h]
        m_st = m_ref[0, h]

        log_inter = log_f + m_st
        m_row = jnp.maximum(log_inter, log_i)
        s = jnp.sum(q * k, axis=-1, keepdims=True) * jnp.exp(log_i - m_row)
        w_inter = jnp.exp(log_inter - m_row)
        q_c = jnp.sum(to_col(q) * c_st, axis=0, keepdims=True)
        num = w_inter * q_c + s * v
        den = w_inter * jnp.sum(q * n_st, axis=-1, keepdims=True) + s
        hcur = num / jnp.maximum(jnp.abs(den), jnp.exp(-m_row))
        gh = gh_ref[:, h * dv:(h + 1) * dv]
        hs_ref[0, :, h * dv:(h + 1) * dv] = _head_out(hcur, gh, o).astype(hs_ref.dtype)

        m_new = jnp.maximum(log_f + m_st, log_i)
        w = jnp.exp(log_i - m_new)
        decay = jnp.exp(log_f + m_st - m_new)
        c_out[0, h] = decay * c_st + (w * to_col(k)) * v
        n_out[0, h] = decay * n_st + w * k
        m_out[0, h] = m_new


def mlstm_sample(qkvo, gates, b_gate, g_head, c0, n0, m0):
    batch, heads, dk, dv = c0.shape
    hk, hv = heads * dk, heads * dv
    width = qkvo.shape[1]
    hs, c_f, n_f, m_f = pl.pallas_call(
        functools.partial(_mlstm_step_kernel, heads=heads, dk=dk, dv=dv),
        grid=(batch,),
        in_specs=[
            pl.BlockSpec((1, 1, width), lambda b: (b, 0, 0)),
            pl.BlockSpec((1, 1, LANES), lambda b: (b, 0, 0)),
            pl.BlockSpec((1, LANES), lambda b: (0, 0)),
            pl.BlockSpec((1, hv), lambda b: (0, 0)),
            pl.BlockSpec((1, heads, dk, dv), lambda b: (b, 0, 0, 0)),
            pl.BlockSpec((1, heads, 1, dk), lambda b: (b, 0, 0, 0)),
            pl.BlockSpec((1, heads, 1, 1), lambda b: (b, 0, 0, 0)),
        ],
        out_specs=[
            pl.BlockSpec((1, 1, hv), lambda b: (b, 0, 0)),
            pl.BlockSpec((1, heads, dk, dv), lambda b: (b, 0, 0, 0)),
            pl.BlockSpec((1, heads, 1, dk), lambda b: (b, 0, 0, 0)),
            pl.BlockSpec((1, heads, 1, 1), lambda b: (b, 0, 0, 0)),
        ],
        out_shape=[
            jax.ShapeDtypeStruct((batch, 1, hv), BF16),
            jax.ShapeDtypeStruct((batch, heads, dk, dv), F32),
            jax.ShapeDtypeStruct((batch, heads, 1, dk), F32),
            jax.ShapeDtypeStruct((batch, heads, 1, 1), F32),
        ],
        compiler_params=_params("parallel"),
        name="mlstm_step",
    )(qkvo.reshape(batch, 1, width), gates.reshape(batch, 1, LANES),
      jnp.pad(b_gate, (0, LANES - 2 * heads)).reshape(1, LANES), g_head.reshape(1, hv),
      c0, n0.reshape(batch, heads, 1, dk), m0.reshape(batch, heads, 1, 1))
    return hs.reshape(batch, hv), c_f, n_f.reshape(batch, heads, dk), m_f.reshape(batch, heads)


def _suffix_matrix(tk):
    j = jnp.arange(tk)[:, None]
    s = jnp.arange(tk)[None, :]
    half = jnp.concatenate([(j >= s).astype(BF16), jnp.ones((tk, tk), BF16)], axis=1)
    return jnp.concatenate([half, half], axis=0)


def _sb_scan(z, mask, suffix, tk):
    n = z.shape[1] // tk
    neg = -z
    log_om = jnp.minimum(neg, 0.0) - jnp.log(1.0 + jnp.exp(jnp.minimum(z, neg)))
    sums = []
    for c in range(n):
        part = log_om[:, c * tk:(c + 1) * tk]
        if mask is not None and c == n - 1:
            part = jnp.where(mask, part, 0.0)
        hi, lo = _split_bf16(part)
        r = _dot(jnp.concatenate([hi, lo], axis=1), suffix)
        sums.append((r[:, :tk], r[:, tk:]))
    return sums


def _sb_weights(z, sums, mask, carry, tk):
    n = len(sums)
    args = [None] * n
    for c in range(n - 1, -1, -1):
        within, total = sums[c]
        args[c] = (z[:, c * tk:(c + 1) * tk] + carry) + within
        if mask is not None and c == n - 1:
            args[c] = jnp.where(mask, args[c], -jnp.inf)
        carry = carry + total
    return jnp.exp(jnp.concatenate(args, axis=1)), carry


def _sb_block(z, mask, suffix, carry, tk):
    return _sb_weights(z, _sb_scan(z, mask, suffix, tk), mask, carry, tk)


def _sb_prompt_kernel(bias_ref, q_ref, k_ref, v_ref, suffix_ref, o_ref, kb_ref, vb_ref,
                      *, tq, tk, nq, group, scale):
    hg = pl.program_id(1)
    i = pl.program_id(2)
    n_sub = tq // tk
    hd = q_ref.shape[1] // group

    def head_cols(g):
        return slice(g * hd, (g + 1) * hd)

    @pl.when(i == 0)
    def _():
        rows = kb_ref.shape[1]
        lane = lax.broadcasted_iota(jnp.int32, (rows, hd), 1)
        for g in range(group):
            kb_ref[g, :, :hd] = (k_ref[:, head_cols(g)] * scale).astype(BF16)
            b0 = jnp.full((rows, hd), bias_ref[hg * group + g], F32)
            p0 = b0.astype(BF16).astype(F32)
            b1 = b0 - p0
            p1 = b1.astype(BF16).astype(F32)
            p2 = b1 - p1
            pieces = jnp.where(lane == 0, p0, jnp.where(lane == 1, p1, jnp.where(lane == 2, p2, 0.0)))
            kb_ref[g, :, hd:] = pieces.astype(BF16)
            vb_ref[g] = v_ref[:, head_cols(g)].astype(BF16)

    suffix = suffix_ref[...]
    strict = (lax.broadcasted_iota(jnp.int32, (tk, tk), 1)
              < lax.broadcasted_iota(jnp.int32, (tk, tk), 0))
    ones = jnp.ones((tq, hd), BF16)
    q_aug = [jnp.concatenate([q_ref[:, head_cols(g)], ones], axis=1) for g in range(group)]

    def logits(start):
        return jnp.concatenate([_dot_nt(q_aug[g], kb_ref[g, start:start + tq, :])
                                for g in range(group)], axis=0)

    own = [(g * tq + r * tk, (r + 1) * tk) for g in range(group) for r in range(n_sub)]

    def scan(b, z):
        if b == 0:
            return [_sb_scan(z[row:row + tk, :width], strict, suffix, tk) for row, width in own]
        return _sb_scan(z, None, suffix, tk)

    def weigh(b, z, sums, carry):
        if b > 0:
            return _sb_weights(z, sums, None, carry, tk)
        weights, carries = [], []
        for (row, width), s in zip(own, sums):
            a, c_r = _sb_weights(z[row:row + tk, :width], s, strict, jnp.zeros((tk, tk), F32), tk)
            if width < tq:
                a = jnp.concatenate([a, jnp.zeros((tk, tq - width), F32)], axis=1)
            weights.append(a)
            carries.append(c_r)
        return jnp.concatenate(weights, axis=0), jnp.concatenate(carries, axis=0)

    for blk in range(nq):
        @pl.when(i == blk)
        def _(blk=blk):
            starts = [(blk - b) * tq for b in range(blk + 1)]
            n_blk = len(starts)
            zs = {b: logits(starts[b]) for b in range(min(2, n_blk))}
            sums = {0: scan(0, zs[0])}
            carry = None
            acc = [0.0] * group
            for b in range(n_blk):
                if b + 1 < n_blk:
                    sums[b + 1] = scan(b + 1, zs[b + 1])
                a, carry = weigh(b, zs.pop(b), sums.pop(b), carry)
                a = a.astype(BF16)
                for g in range(group):
                    acc[g] = acc[g] + _dot(a[g * tq:(g + 1) * tq],
                                           vb_ref[g, starts[b]:starts[b] + tq, :])
                if b + 2 < n_blk:
                    zs[b + 2] = logits(starts[b + 2])
            for g in range(group):
                o_ref[:, head_cols(g)] = acc[g].astype(o_ref.dtype)


def sb_prompt_attention(q, k, v, logit_bias, batch, seq, heads, hd):
    tk = SB_TK
    tq = SB_TQ if seq % SB_TQ == 0 else tk
    group = SB_HEAD_GROUP if heads % SB_HEAD_GROUP == 0 else 1
    assert tq % tk == 0 and seq % tq == 0 and hd == LANES
    nq = seq // tq
    return pl.pallas_call(
        functools.partial(_sb_prompt_kernel, tq=tq, tk=tk, nq=nq, group=group, scale=hd ** -0.5),
        grid=(batch, heads // group, nq),
        in_specs=[
            pl.BlockSpec(memory_space=pltpu.SMEM),
            pl.BlockSpec((tq, group * hd), lambda b, h, i: (b * nq + i, h)),
            pl.BlockSpec((seq, group * hd), lambda b, h, i: (b, h)),
            pl.BlockSpec((seq, group * hd), lambda b, h, i: (b, h)),
            pl.BlockSpec((2 * tk, 2 * tk), lambda b, h, i: (0, 0)),
        ],
        out_specs=pl.BlockSpec((tq, group * hd), lambda b, h, i: (b * nq + i, h)),
        out_shape=jax.ShapeDtypeStruct((batch * seq, heads * hd), BF16),
        scratch_shapes=[pltpu.VMEM((group, seq, 2 * hd), BF16), pltpu.VMEM((group, seq, hd), BF16)],
        compiler_params=_params("parallel", "parallel", "arbitrary"),
        name="sb_prompt",
    )(logit_bias, q, k, v, _suffix_matrix(tk))


def _decode_phases(in_refs, o_ref, scratch_refs, *, pages, heads, hd, page, past_len, scale):
    q_ref, kn_ref, vn_ref, bias_ref, suffix_ref = in_refs[:5]
    k_refs = in_refs[5:5 + pages]
    v_refs = in_refs[5 + pages:5 + 2 * pages]
    qbd_ref, carry_ref, acc_ref = scratch_refs
    width = heads * hd

    def own_columns():
        head_of_col = lax.broadcasted_iota(jnp.int32, (heads, width), 1) // hd
        return head_of_col == lax.broadcasted_iota(jnp.int32, (heads, width), 0)

    def init():
        own = own_columns()
        bias = bias_ref[...]
        q_bd = jnp.where(own, q_ref[0].astype(F32), 0.0)
        qbd_ref[...] = q_bd.astype(BF16)
        k_new = kn_ref[0]
        v_new = vn_ref[0]
        z_new = jnp.sum(q_bd.astype(F32) * k_new, axis=-1, keepdims=True) * scale + bias
        visible = jnp.full((heads, 1), past_len, jnp.int32) < past_len
        sp = _softplus(z_new)
        log_om = jnp.where(visible, -sp, 0.0)
        a_new = jnp.where(visible, jnp.exp(z_new - sp), 0.0)
        carry_ref[...] = jnp.broadcast_to(log_om, carry_ref.shape)
        acc_ref[...] = a_new * v_new

    def token_rows(refs):
        blocks = []
        for ref in reversed(refs):
            x = jnp.swapaxes(ref[0].reshape(page, heads, hd).astype(BF16), 0, 1)
            blocks.append(jnp.concatenate([x[hh] for hh in range(heads)], axis=1))
        return jnp.concatenate(blocks, axis=0)

    def body():
        keys = token_rows(k_refs)
        values = token_rows(v_refs)
        z = _dot_nt(qbd_ref[...], keys) * scale + bias_ref[...]
        a, carry = _sb_block(z, None, suffix_ref[...], carry_ref[...], page)
        carry_ref[...] = carry
        acc_ref[...] += _dot(a.astype(BF16), values)

    def finish():
        acc = jnp.where(own_columns(), acc_ref[...], 0.0)
        o_ref[0] = jnp.sum(acc, axis=0, keepdims=True).astype(o_ref.dtype)

    return init, body, finish


def _decode_call_parts(q, k_new, v_new, k_pool, v_pool, first_page, page_table, logit_bias,
                       heads, hd, steps):
    batch, width = q.shape
    page = k_pool.shape[1] // heads
    n_pages = page_table.shape[1]
    assert page == LANES and hd == LANES and n_pages % steps == 0
    pages = n_pages // steps

    def page_spec(idx):
        return pl.BlockSpec(
            (1, page * heads, hd),
            lambda b, p, pt, idx=idx: (first_page + pt[b, n_pages - 1 - (p * pages + idx)], 0, 0))

    vec = pl.BlockSpec((1, 1, width), lambda b, p, pt: (b, 0, 0))
    static = dict(pages=pages, heads=heads, hd=hd, page=page, past_len=n_pages * page,
                  scale=hd ** -0.5)
    in_specs = ([vec, vec, vec,
                 pl.BlockSpec((heads, 1), lambda b, p, pt: (0, 0)),
                 pl.BlockSpec((2 * page, 2 * page), lambda b, p, pt: (0, 0))]
                + [page_spec(i) for i in range(pages)] * 2)
    args = [q.reshape(batch, 1, width), k_new.reshape(batch, 1, width),
            v_new.reshape(batch, 1, width), logit_bias.reshape(heads, 1), _suffix_matrix(page),
            *([k_pool] * pages), *([v_pool] * pages)]
    scratch = [pltpu.VMEM((heads, width), BF16), pltpu.VMEM((heads, page), F32),
               pltpu.VMEM((heads, width), F32)]
    return static, in_specs, args, vec, jax.ShapeDtypeStruct((batch, 1, width), BF16), scratch


def _sb_decode_kernel(pt_ref, *refs, n_in, **static):
    p = pl.program_id(1)
    init, body, finish = _decode_phases(refs[:n_in], refs[n_in], refs[n_in + 1:], **static)
    pl.when(p == 0)(init)
    body()
    pl.when(p == pl.num_programs(1) - 1)(finish)


def sb_decode_attention(q, k_new, v_new, k_pool, v_pool, first_page, page_table, logit_bias,
                        heads, hd):
    n_pages = page_table.shape[1]
    pages = DEC_PAGES_PER_STEP if n_pages % DEC_PAGES_PER_STEP == 0 else 1
    static, in_specs, args, out_spec, out_shape, scratch = _decode_call_parts(
        q, k_new, v_new, k_pool, v_pool, first_page, page_table, logit_bias, heads, hd,
        n_pages // pages)
    out = pl.pallas_call(
        functools.partial(_sb_decode_kernel, n_in=len(in_specs), **static),
        grid_spec=pltpu.PrefetchScalarGridSpec(
            num_scalar_prefetch=1, grid=(q.shape[0], n_pages // pages),
            in_specs=in_specs, out_specs=out_spec, scratch_shapes=scratch),
        out_shape=out_shape,
        compiler_params=_params("parallel", "arbitrary"),
        name="sb_decode",
    )(page_table, *args)
    return out.reshape(q.shape)


def _mlp_phases(y_ref, g_ref, wu_ref, wd_ref, gf_ref, out_ref, h_ref):
    def init():
        y = y_ref[...]
        h_ref[...] = _rmsnorm_rows(y, g_ref[...]).astype(BF16)
        out_ref[...] = y

    def body():
        u = jnp.maximum(_dot(h_ref[...], wu_ref[...]), 0.0)
        out_ref[...] += _dot((u * u).astype(BF16), wd_ref[...])

    def finish():
        if gf_ref is not None:
            out_ref[...] = _rmsnorm_rows(out_ref[...], gf_ref[...])

    return init, body, finish


def _mlp_kernel(*refs, final_norm, n_dec_in, dec_static):
    if n_dec_in:
        refs = refs[1:]
    n_mlp_in = 5 if final_norm else 4
    mlp_in = list(refs[:n_mlp_in]) + ([] if final_norm else [None])
    dec_in = refs[n_mlp_in:n_mlp_in + n_dec_in]
    pos = n_mlp_in + n_dec_in
    out_ref = refs[pos]
    pos += 1
    phases = []
    if n_dec_in:
        phases.append(_decode_phases(dec_in, refs[pos], refs[pos + 2:], **dec_static))
        pos += 1
    phases.insert(0, _mlp_phases(*mlp_in, out_ref, refs[pos]))

    def run_all(which):
        def run():
            for phase in phases:
                phase[which]()
        return run

    f = pl.program_id(1)
    pl.when(f == 0)(run_all(0))
    run_all(1)()
    pl.when(f == pl.num_programs(1) - 1)(run_all(2))


def mlp_residual(y, g, w_up, w_down, final_g=None, decode=None, *, tm=MLP_TM, tf=MLP_TF):
    m, d = y.shape
    d_ff = w_up.shape[1]
    tm = _row_tile(m, tm)
    assert d_ff % tf == 0
    grid = (m // tm, d_ff // tf)
    in_specs = [
        pl.BlockSpec((tm, d), lambda i, f, *_: (i, 0),
                     **({"pipeline_mode": pl.Buffered(1)} if decode else {})),
        pl.BlockSpec((1, d), lambda i, f, *_: (0, 0)),
        pl.BlockSpec((d, tf), lambda i, f, *_: (0, f)),
        pl.BlockSpec((tf, d), lambda i, f, *_: (f, 0)),
    ]
    args = [y, g.reshape(1, d), w_up, w_down]
    if final_g is not None:
        in_specs.append(pl.BlockSpec((1, d), lambda i, f, *_: (0, 0)))
        args.append(final_g.reshape(1, d))
    out_specs = [pl.BlockSpec((tm, d), lambda i, f, *_: (i, 0))]
    out_shape = [jax.ShapeDtypeStruct((m, d), F32)]
    scratch = [pltpu.VMEM((tm, d), BF16)]
    dec_static, n_dec_in, prefetch = None, 0, []
    if decode:
        assert decode["q"].shape[0] == grid[0]
        dec_static, dec_specs, dec_args, o_spec, o_shape, dec_scratch = _decode_call_parts(
            steps=grid[1], **decode)
        n_dec_in = len(dec_specs)
        in_specs += dec_specs
        args += dec_args
        out_specs.append(o_spec)
        out_shape.append(o_shape)
        scratch += dec_scratch
        prefetch = [decode["page_table"]]
    outs = pl.pallas_call(
        functools.partial(_mlp_kernel, final_norm=final_g is not None, n_dec_in=n_dec_in,
                          dec_static=dec_static),
        grid_spec=pltpu.PrefetchScalarGridSpec(
            num_scalar_prefetch=len(prefetch), grid=grid, in_specs=in_specs,
            out_specs=out_specs, scratch_shapes=scratch),
        out_shape=out_shape,
        compiler_params=_params("parallel", "arbitrary"),
        name="mlp_residual",
    )(*prefetch, *args)
    if decode:
        return outs[0], outs[1].reshape(decode["q"].shape)
    return outs[0]


def kernel(x_prompt, x_sample, state_C, state_n, state_m, cache_k, cache_v, page_table,
           norm_mix_g, norm_ffn_g, ml_w_in, ml_b_gate, ml_head_g, ml_w_out,
           sb_w_in, sb_logit_bias, sb_w_out, ffn_w_up, ffn_w_down, final_g):
    batch, seq, d = x_prompt.shape
    dec_batch, dec_seq, _ = x_sample.shape
    assert dec_seq == 1
    assert norm_mix_g.shape[0] == 2
    ml_heads, ml_dk, ml_dv = state_C.shape[2:]
    sb_heads, sb_hd = cache_k.shape[3:]
    hk, hv = ml_heads * ml_dk, ml_heads * ml_dv
    ml_main = 2 * hk + 2 * hv
    m = batch * seq

    ml_in = ml_w_in[0].astype(BF16)
    ml_gate = jnp.pad(ml_in[:, ml_main:], ((0, 0), (0, LANES - 2 * ml_heads)))
    ml_out = ml_w_out[0].astype(BF16)
    sb_in = sb_w_in[0].astype(BF16)
    sb_out = sb_w_out[0].astype(BF16)
    w_up = ffn_w_up.astype(BF16)
    w_down = ffn_w_down.astype(BF16)
    ml_segs = [(ml_main, BF16)]
    sb_segs = [(d, BF16), (d, F32), (d, F32)]

    ys = x_sample.reshape(dec_batch, d)
    qkvo, gates = norm_matmul(ys, norm_mix_g[0], ml_in, ml_segs, w_gate=ml_gate)
    hs, c_s, n_s, m_s = mlstm_sample(qkvo, gates, ml_b_gate[0], ml_head_g[0],
                                     state_C[0], state_n[0], state_m[0])
    ys = proj_residual(hs, ml_out, ys)
    ys = mlp_residual(ys, norm_ffn_g[0], w_up[0], w_down[0])
    q_s, k_s, v_s = norm_matmul(ys, norm_mix_g[1], sb_in, sb_segs)

    n_layers, n_pool, page = cache_k.shape[:3]
    pool_shape = (n_layers * n_pool, page * sb_heads, sb_hd)
    pool = dict(k_pool=cache_k.reshape(pool_shape), v_pool=cache_v.reshape(pool_shape),
                first_page=0, logit_bias=sb_logit_bias[0], heads=sb_heads, hd=sb_hd)
    n_hosts = 2
    per_host = dec_batch // n_hosts
    ride = (dec_batch % n_hosts == 0 and per_host == m // _row_tile(m, MLP_TM)
            and page_table.shape[1] % (w_up.shape[2] // MLP_TF) == 0)

    def decode_share(r):
        rows = slice(r * per_host, (r + 1) * per_host)
        return dict(q=q_s[rows], k_new=k_s[rows], v_new=v_s[rows], page_table=page_table[rows],
                    **pool)

    yp = x_prompt.reshape(m, d)
    qkvo, gates = norm_matmul(yp, norm_mix_g[0], ml_in, ml_segs, w_gate=ml_gate)
    hs, c_p, n_p, m_p = mlstm_prompt(qkvo, gates, ml_b_gate[0], ml_head_g[0],
                                     batch, seq, ml_heads, ml_dk, ml_dv)
    yp = proj_residual(hs, ml_out, yp)
    if ride:
        yp, o_first = mlp_residual(yp, norm_ffn_g[0], w_up[0], w_down[0], decode=decode_share(0))
    else:
        yp = mlp_residual(yp, norm_ffn_g[0], w_up[0], w_down[0])

    q_p, k_p, v_p = norm_matmul(yp, norm_mix_g[1], sb_in, sb_segs)
    o = sb_prompt_attention(q_p, k_p, v_p, sb_logit_bias[0], batch, seq, sb_heads, sb_hd)
    yp = proj_residual(o, sb_out, yp)
    if ride:
        yp, o_second = mlp_residual(yp, norm_ffn_g[1], w_up[1], w_down[1], final_g,
                                    decode=decode_share(1))
        o_s = jnp.concatenate([o_first, o_second], axis=0)
    else:
        yp = mlp_residual(yp, norm_ffn_g[1], w_up[1], w_down[1], final_g)
        o_s = sb_decode_attention(q_s, k_s, v_s, page_table=page_table, **pool)

    ys = proj_residual(o_s, sb_out, ys)
    ys = mlp_residual(ys, norm_ffn_g[1], w_up[1], w_down[1], final_g)

    kv_p = (1, batch, seq, sb_heads, sb_hd)
    kv_s = (1, dec_batch, dec_seq, sb_heads, sb_hd)
    return (yp.reshape(batch, seq, d), ys.reshape(dec_batch, dec_seq, d),
            c_p[None], n_p[None], m_p[None], k_p.reshape(kv_p), v_p.reshape(kv_p),
            c_s[None], n_s[None], m_s[None], k_s.reshape(kv_s), v_s.reshape(kv_s))
```

```python
import functools

import jax
import jax.numpy as jnp
from jax import lax
from jax.experimental import pallas as pl
from jax.experimental.pallas import tpu as pltpu

F32 = jnp.float32
BF16 = jnp.bfloat16

EPS = 1e-6
GATE_CAP = 15.0
LANES = 128
VMEM_LIMIT_BYTES = 56 * 1024 * 1024

ML_CHUNK = 128
SB_TQ = 512
SB_TK = 128
SB_HEAD_GROUP = 4
DEC_PAGES_PER_STEP = 4
MLP_TM = 1024
MLP_TF = 512


def _params(*sem):
    return pltpu.CompilerParams(dimension_semantics=sem, vmem_limit_bytes=VMEM_LIMIT_BYTES)


def _row_tile(m, want):
    return want if m % want == 0 else m


def _rmsnorm_rows(x, g):
    return x * lax.rsqrt(jnp.mean(x * x, axis=-1, keepdims=True) + EPS) * g


def _log_sigmoid(x):
    return jnp.minimum(x, 0.0) - jnp.log1p(jnp.exp(-jnp.abs(x)))


def _softplus(x):
    return jnp.maximum(x, 0.0) + jnp.log1p(jnp.exp(-jnp.abs(x)))


def _split_bf16(x):
    hi = x.astype(BF16)
    lo = (x - hi.astype(F32)).astype(BF16)
    return hi, lo


def _dot(a, b):
    return jnp.dot(a, b, preferred_element_type=F32)


def _dot_nt(a, b):
    return lax.dot_general(a, b, (((1,), (1,)), ((), ())), preferred_element_type=F32)


def _dot_tn(a, b):
    return lax.dot_general(a, b, (((0,), (0,)), ((), ())), preferred_element_type=F32)


def _norm_matmul_kernel(*refs, seg_tiles, has_gate):
    x_ref, g_ref, w_ref = refs[:3]
    pos = 3
    wg_ref = None
    if has_gate:
        wg_ref = refs[pos]
        pos += 1
    n_seg = len(seg_tiles)
    out_refs = refs[pos:pos + n_seg]
    pos += n_seg
    gate_ref = None
    if has_gate:
        gate_ref = refs[pos]
        pos += 1
    h_ref = refs[pos]

    j = pl.program_id(1)

    @pl.when(j == 0)
    def _():
        h_ref[...] = _rmsnorm_rows(x_ref[...], g_ref[...]).astype(BF16)
        if has_gate:
            gate_ref[...] = _dot(h_ref[...], wg_ref[...])

    acc = _dot(h_ref[...], w_ref[...])
    lo = 0
    for out_ref, nt in zip(out_refs, seg_tiles):
        @pl.when((j >= lo) & (j < lo + nt))
        def _(out_ref=out_ref):
            out_ref[...] = acc.astype(out_ref.dtype)
        lo += nt


def norm_matmul(x, g, w, segs, *, w_gate=None, tm=1024, tn=1024):
    m, k = x.shape
    tm = _row_tile(m, tm)
    seg_tiles = tuple(n // tn for n, _ in segs)
    assert all(n % tn == 0 for n, _ in segs) and m % tm == 0
    starts = []
    lo = 0
    for nt in seg_tiles:
        starts.append(lo)
        lo += nt
    n_tiles = lo

    in_specs = [
        pl.BlockSpec((tm, k), lambda i, j: (i, 0)),
        pl.BlockSpec((1, k), lambda i, j: (0, 0)),
        pl.BlockSpec((k, tn), lambda i, j: (0, j)),
    ]
    args = [x, g.reshape(1, k), w]
    if w_gate is not None:
        in_specs.append(pl.BlockSpec((k, LANES), lambda i, j: (0, 0)))
        args.append(w_gate)

    out_shape, out_specs = [], []
    for (n, dt), nt, st in zip(segs, seg_tiles, starts):
        out_shape.append(jax.ShapeDtypeStruct((m, n), dt))
        out_specs.append(pl.BlockSpec(
            (tm, tn), lambda i, j, st=st, nt=nt: (i, jnp.clip(j - st, 0, nt - 1))))
    if w_gate is not None:
        out_shape.append(jax.ShapeDtypeStruct((m, LANES), F32))
        out_specs.append(pl.BlockSpec((tm, LANES), lambda i, j: (i, 0)))

    return pl.pallas_call(
        functools.partial(_norm_matmul_kernel, seg_tiles=seg_tiles, has_gate=w_gate is not None),
        grid=(m // tm, n_tiles),
        in_specs=in_specs,
        out_specs=out_specs,
        out_shape=out_shape,
        scratch_shapes=[pltpu.VMEM((tm, k), BF16)],
        compiler_params=_params("parallel", "arbitrary"),
        name="norm_matmul",
    )(*args)


def _proj_residual_kernel(a_ref, w_ref, y_ref, out_ref):
    out_ref[...] = y_ref[...] + _dot(a_ref[...], w_ref[...])


def proj_residual(a, w, y, *, tm=512):
    m, k = a.shape
    n = w.shape[1]
    tm = _row_tile(m, tm)
    return pl.pallas_call(
        _proj_residual_kernel,
        grid=(m // tm,),
        in_specs=[
            pl.BlockSpec((tm, k), lambda i: (i, 0)),
            pl.BlockSpec((k, n), lambda i: (0, 0)),
            pl.BlockSpec((tm, n), lambda i: (i, 0)),
        ],
        out_specs=pl.BlockSpec((tm, n), lambda i: (i, 0)),
        out_shape=jax.ShapeDtypeStruct((m, n), F32),
        compiler_params=_params("parallel"),
        name="proj_residual",
    )(a, w, y)


def _gate_act(pre):
    capped = GATE_CAP * jnp.tanh(pre / GATE_CAP)
    return capped, _log_sigmoid(capped)


def _head_out(hcur, gh, o):
    hn = hcur * lax.rsqrt(jnp.mean(hcur * hcur, axis=-1, keepdims=True) + EPS)
    return hn * gh * jax.nn.sigmoid(o)


def _mlstm_chunk_kernel(q_ref, k_ref, v_ref, o_ref, gc_ref, gr_ref, bc_ref, br_ref, gh_ref,
                        tri_ref, tril_ref, hs_ref, c_ref, n_ref, m_ref, *, heads, dk, dv, chunk):
    c_idx = pl.program_id(1)

    @pl.when(c_idx == 0)
    def _():
        c_ref[...] = jnp.zeros_like(c_ref)
        n_ref[...] = jnp.zeros_like(n_ref)
        m_ref[...] = jnp.zeros_like(m_ref)

    k_scale = dk ** -0.5
    tri = tri_ref[...]
    tril = tril_ref[...]
    li_c, lf_c = _gate_act(gc_ref[...] + br_ref[...])
    li_r, lf_r = _gate_act(gr_ref[...] + bc_ref[...])
    hi, lo = _split_bf16(lf_c)
    b_c = _dot(tril, hi) + _dot(tril, lo)
    hi, lo = _split_bf16(lf_r)
    b_r = _dot(hi, tri) + _dot(lo, tri)

    row = lax.broadcasted_iota(jnp.int32, (chunk, chunk), 0)
    col = lax.broadcasted_iota(jnp.int32, (chunk, chunk), 1)
    causal = col <= row

    for h in range(heads):
        q = q_ref[:, h * dk:(h + 1) * dk]
        k = k_ref[:, h * dk:(h + 1) * dk] * k_scale
        v = v_ref[:, h * dv:(h + 1) * dv]
        i_col = li_c[:, h:h + 1]
        b_col = b_c[:, heads + h:heads + h + 1]
        i_row = li_r[h:h + 1, :]
        b_row = b_r[heads + h:heads + h + 1, :]
        m_st = m_ref[0, h]
        c_st = c_ref[0, h]
        n_st = n_ref[0, h]

        log_d = jnp.where(causal, (b_col - b_row) + i_row, -jnp.inf)
        log_inter = b_col + m_st
        m_row = jnp.maximum(log_inter, jnp.max(log_d, axis=-1, keepdims=True))
        s = _dot_nt(q, k) * jnp.exp(log_d - m_row)
        w_inter = jnp.exp(log_inter - m_row)
        qf = q.astype(F32)
        num = w_inter * _dot(q, c_st.astype(BF16)) + _dot(s.astype(BF16), v)
        den = (w_inter * jnp.sum(qf * n_st, axis=-1, keepdims=True)
               + jnp.sum(s, axis=-1, keepdims=True))
        hcur = num / jnp.maximum(jnp.abs(den), jnp.exp(-m_row))
        gh = gh_ref[:, h * dv:(h + 1) * dv]
        o = o_ref[:, h * dv:(h + 1) * dv].astype(F32)
        hs_ref[:, h * dv:(h + 1) * dv] = _head_out(hcur, gh, o).astype(hs_ref.dtype)

        b_last = b_col[chunk - 1:chunk, :]
        log_w = (b_last - b_col) + i_col
        m_new = jnp.maximum(b_last + m_st, jnp.max(log_w, axis=0, keepdims=True))
        w = jnp.exp(log_w - m_new)
        decay = jnp.exp(b_last + m_st - m_new)
        wv = (w * v.astype(F32)).astype(BF16)
        c_ref[0, h] = decay * c_st + _dot_tn(k, wv)
        n_ref[0, h] = decay * n_st + jnp.sum(w * k.astype(F32), axis=0, keepdims=True)
        m_ref[0, h] = m_new


def mlstm_prompt(qkvo, gates, b_gate, g_head, batch, seq, heads, dk, dv):
    chunk = ML_CHUNK if seq % ML_CHUNK == 0 else seq
    n_chunks = seq // chunk
    hk, hv = heads * dk, heads * dv
    assert hk % LANES == 0 and hv % hk == 0
    kv_ratio = hv // hk
    gates_r = gates[:, :2 * heads].T
    tri = (jnp.arange(chunk)[:, None] <= jnp.arange(chunk)[None, :]).astype(BF16)

    def rows(b, c):
        return b * n_chunks + c

    hs, c_f, n_f, m_f = pl.pallas_call(
        functools.partial(_mlstm_chunk_kernel, heads=heads, dk=dk, dv=dv, chunk=chunk),
        grid=(batch, n_chunks),
        in_specs=[
            pl.BlockSpec((chunk, hk), lambda b, c: (rows(b, c), 0)),
            pl.BlockSpec((chunk, hk), lambda b, c: (rows(b, c), 1)),
            pl.BlockSpec((chunk, hv), lambda b, c: (rows(b, c), 2 // kv_ratio)),
            pl.BlockSpec((chunk, hv), lambda b, c: (rows(b, c), 2 // kv_ratio + 1)),
            pl.BlockSpec((chunk, LANES), lambda b, c: (rows(b, c), 0)),
            pl.BlockSpec((2 * heads, chunk), lambda b, c: (0, rows(b, c))),
            pl.BlockSpec((2 * heads, 1), lambda b, c: (0, 0)),
            pl.BlockSpec((1, LANES), lambda b, c: (0, 0)),
            pl.BlockSpec((1, hv), lambda b, c: (0, 0)),
            pl.BlockSpec((chunk, chunk), lambda b, c: (0, 0)),
            pl.BlockSpec((chunk, chunk), lambda b, c: (0, 0)),
        ],
        out_specs=[
            pl.BlockSpec((chunk, hv), lambda b, c: (rows(b, c), 0)),
            pl.BlockSpec((1, heads, dk, dv), lambda b, c: (b, 0, 0, 0)),
            pl.BlockSpec((1, heads, 1, dk), lambda b, c: (b, 0, 0, 0)),
            pl.BlockSpec((1, heads, 1, 1), lambda b, c: (b, 0, 0, 0)),
        ],
        out_shape=[
            jax.ShapeDtypeStruct((batch * seq, hv), BF16),
            jax.ShapeDtypeStruct((batch, heads, dk, dv), F32),
            jax.ShapeDtypeStruct((batch, heads, 1, dk), F32),
            jax.ShapeDtypeStruct((batch, heads, 1, 1), F32),
        ],
        compiler_params=_params("parallel", "arbitrary"),
        name="mlstm_chunk",
    )(qkvo, qkvo, qkvo, qkvo, gates, gates_r,
      b_gate.reshape(2 * heads, 1),
      jnp.pad(b_gate, (0, LANES - 2 * heads)).reshape(1, LANES),
      g_head.reshape(1, hv), tri, tri.T)
    return hs, c_f, n_f.reshape(batch, heads, dk), m_f.reshape(batch, heads)


def _mlstm_step_kernel(qkvo_ref, gate_ref, br_ref, gh_ref, c_ref, n_ref, m_ref,
                       hs_ref, c_out, n_out, m_out, *, heads, dk, dv):
    hk, hv = heads * dk, heads * dv
    k_scale = dk ** -0.5
    li, lf = _gate_act(gate_ref[0] + br_ref[...])
    eye = (lax.broadcasted_iota(jnp.int32, (dk, dk), 0)
           == lax.broadcasted_iota(jnp.int32, (dk, dk), 1))

    def to_col(x_row):
        return jnp.sum(jnp.where(eye, x_row, 0.0), axis=-1, keepdims=True)

    for h in range(heads):
        q = qkvo_ref[0, :, h * dk:(h + 1) * dk].astype(F32)
        k = qkvo_ref[0, :, hk + h * dk:hk + (h + 1) * dk].astype(F32) * k_scale
        v = qkvo_ref[0, :, 2 * hk + h * dv:2 * hk + (h + 1) * dv].astype(F32)
        o = qkvo_ref[0, :, 2 * hk + hv + h * dv:2 * hk + hv + (h + 1) * dv].astype(F32)
        log_i = li[:, h:h + 1]
        log_f = lf[:, heads + h:heads + h + 1]
        c_st = c_ref[0, h]
        n_st = n_ref[0, h]
        m_st = m_ref[0, h]

        log_inter = log_f + m_st
        m_row = jnp.maximum(log_inter, log_i)
        s = jnp.sum(q * k, axis=-1, keepdims=True) * jnp.exp(log_i - m_row)
        w_inter = jnp.exp(log_inter - m_row)
        q_c = jnp.sum(to_col(q) * c_st, axis=0, keepdims=True)
        num = w_inter * q_c + s * v
        den = w_inter * jnp.sum(q * n_st, axis=-1, keepdims=True) + s
        hcur = num / jnp.maximum(jnp.abs(den), jnp.exp(-m_row))
        gh = gh_ref[:, h * dv:(h + 1) * dv]
        hs_ref[0, :, h * dv:(h + 1) * dv] = _head_out(hcur, gh, o).astype(hs_ref.dtype)

        m_new = jnp.maximum(log_f + m_st, log_i)
        w = jnp.exp(log_i - m_new)
        decay = jnp.exp(log_f + m_st - m_new)
        c_out[0, h] = decay * c_st + (w * to_col(k)) * v
        n_out[0, h] = decay * n_st + w * k
        m_out[0, h] = m_new


def mlstm_sample(qkvo, gates, b_gate, g_head, c0, n0, m0):
    batch, heads, dk, dv = c0.shape
    hk, hv = heads * dk, heads * dv
    width = qkvo.shape[1]
    hs, c_f, n_f, m_f = pl.pallas_call(
        functools.partial(_mlstm_step_kernel, heads=heads, dk=dk, dv=dv),
        grid=(batch,),
        in_specs=[
            pl.BlockSpec((1, 1, width), lambda b: (b, 0, 0)),
            pl.BlockSpec((1, 1, LANES), lambda b: (b, 0, 0)),
            pl.BlockSpec((1, LANES), lambda b: (0, 0)),
            pl.BlockSpec((1, hv), lambda b: (0, 0)),
            pl.BlockSpec((1, heads, dk, dv), lambda b: (b, 0, 0, 0)),
            pl.BlockSpec((1, heads, 1, dk), lambda b: (b, 0, 0, 0)),
            pl.BlockSpec((1, heads, 1, 1), lambda b: (b, 0, 0, 0)),
        ],
        out_specs=[
            pl.BlockSpec((1, 1, hv), lambda b: (b, 0, 0)),
            pl.BlockSpec((1, heads, dk, dv), lambda b: (b, 0, 0, 0)),
            pl.BlockSpec((1, heads, 1, dk), lambda b: (b, 0, 0, 0)),
            pl.BlockSpec((1, heads, 1, 1), lambda b: (b, 0, 0, 0)),
        ],
        out_shape=[
            jax.ShapeDtypeStruct((batch, 1, hv), BF16),
            jax.ShapeDtypeStruct((batch, heads, dk, dv), F32),
            jax.ShapeDtypeStruct((batch, heads, 1, dk), F32),
            jax.ShapeDtypeStruct((batch, heads, 1, 1), F32),
        ],
        compiler_params=_params("parallel"),
        name="mlstm_step",
    )(qkvo.reshape(batch, 1, width), gates.reshape(batch, 1, LANES),
      jnp.pad(b_gate, (0, LANES - 2 * heads)).reshape(1, LANES), g_head.reshape(1, hv),
      c0, n0.reshape(batch, heads, 1, dk), m0.reshape(batch, heads, 1, 1))
    return hs.reshape(batch, hv), c_f, n_f.reshape(batch, heads, dk), m_f.reshape(batch, heads)


def _suffix_matrix(tk):
    j = jnp.arange(tk)[:, None]
    s = jnp.arange(tk)[None, :]
    half = jnp.concatenate([(j >= s).astype(BF16), jnp.ones((tk, tk), BF16)], axis=1)
    return jnp.concatenate([half, half], axis=0)


def _sb_scan(z, mask, suffix, tk):
    n = z.shape[1] // tk
    neg = -z
    log_om = jnp.minimum(neg, 0.0) - jnp.log(1.0 + jnp.exp(jnp.minimum(z, neg)))
    sums = []
    for c in range(n):
        part = log_om[:, c * tk:(c + 1) * tk]
        if mask is not None and c == n - 1:
            part = jnp.where(mask, part, 0.0)
        hi, lo = _split_bf16(part)
        r = _dot(jnp.concatenate([hi, lo], axis=1), suffix)
        sums.append((r[:, :tk], r[:, tk:]))
    return sums


def _sb_weights(z, sums, mask, carry, tk):
    n = len(sums)
    args = [None] * n
    for c in range(n - 1, -1, -1):
        within, total = sums[c]
        args[c] = (z[:, c * tk:(c + 1) * tk] + carry) + within
        if mask is not None and c == n - 1:
            args[c] = jnp.where(mask, args[c], -jnp.inf)
        carry = carry + total
    return jnp.exp(jnp.concatenate(args, axis=1)), carry


def _sb_block(z, mask, suffix, carry, tk):
    return _sb_weights(z, _sb_scan(z, mask, suffix, tk), mask, carry, tk)


def _sb_prompt_kernel(bias_ref, q_ref, k_ref, v_ref, suffix_ref, o_ref, kb_ref, vb_ref,
                      *, tq, tk, nq, group, scale):
    hg = pl.program_id(1)
    i = pl.program_id(2)
    n_sub = tq // tk
    hd = q_ref.shape[1] // group

    def head_cols(g):
        return slice(g * hd, (g + 1) * hd)

    @pl.when(i == 0)
    def _():
        rows = kb_ref.shape[1]
        lane = lax.broadcasted_iota(jnp.int32, (rows, hd), 1)
        for g in range(group):
            kb_ref[g, :, :hd] = (k_ref[:, head_cols(g)] * scale).astype(BF16)
            b0 = jnp.full((rows, hd), bias_ref[hg * group + g], F32)
            p0 = b0.astype(BF16).astype(F32)
            b1 = b0 - p0
            p1 = b1.astype(BF16).astype(F32)
            p2 = b1 - p1
            pieces = jnp.where(lane == 0, p0, jnp.where(lane == 1, p1, jnp.where(lane == 2, p2, 0.0)))
            kb_ref[g, :, hd:] = pieces.astype(BF16)
            vb_ref[g] = v_ref[:, head_cols(g)].astype(BF16)

    suffix = suffix_ref[...]
    strict = (lax.broadcasted_iota(jnp.int32, (tk, tk), 1)
              < lax.broadcasted_iota(jnp.int32, (tk, tk), 0))
    ones = jnp.ones((tq, hd), BF16)
    q_aug = [jnp.concatenate([q_ref[:, head_cols(g)], ones], axis=1) for g in range(group)]

    def logits(start):
        return jnp.concatenate([_dot_nt(q_aug[g], kb_ref[g, start:start + tq, :])
                                for g in range(group)], axis=0)

    own = [(g * tq + r * tk, (r + 1) * tk) for g in range(group) for r in range(n_sub)]

    def scan(b, z):
        if b == 0:
            return [_sb_scan(z[row:row + tk, :width], strict, suffix, tk) for row, width in own]
        return _sb_scan(z, None, suffix, tk)

    def weigh(b, z, sums, carry):
        if b > 0:
            return _sb_weights(z, sums, None, carry, tk)
        weights, carries = [], []
        for (row, width), s in zip(own, sums):
            a, c_r = _sb_weights(z[row:row + tk, :width], s, strict, jnp.zeros((tk, tk), F32), tk)
            if width < tq:
                a = jnp.concatenate([a, jnp.zeros((tk, tq - width), F32)], axis=1)
            weights.append(a)
            carries.append(c_r)
        return jnp.concatenate(weights, axis=0), jnp.concatenate(carries, axis=0)

    for blk in range(nq):
        @pl.when(i == blk)
        def _(blk=blk):
            starts = [(blk - b) * tq for b in range(blk + 1)]
            n_blk = len(starts)
            zs = {b: logits(starts[b]) for b in range(min(2, n_blk))}
            sums = {0: scan(0, zs[0])}
            carry = None
            acc = [0.0] * group
            for b in range(n_blk):
                if b + 1 < n_blk:
                    sums[b + 1] = scan(b + 1, zs[b + 1])
                a, carry = weigh(b, zs.pop(b), sums.pop(b), carry)
                a = a.astype(BF16)
                for g in range(group):
                    acc[g] = acc[g] + _dot(a[g * tq:(g + 1) * tq],
                                           vb_ref[g, starts[b]:starts[b] + tq, :])
                if b + 2 < n_blk:
                    zs[b + 2] = logits(starts[b + 2])
            for g in range(group):
                o_ref[:, head_cols(g)] = acc[g].astype(o_ref.dtype)


def sb_prompt_attention(q, k, v, logit_bias, batch, seq, heads, hd):
    tk = SB_TK
    tq = SB_TQ if seq % SB_TQ == 0 else tk
    group = SB_HEAD_GROUP if heads % SB_HEAD_GROUP == 0 else 1
    assert tq % tk == 0 and seq % tq == 0 and hd == LANES
    nq = seq // tq
    return pl.pallas_call(
        functools.partial(_sb_prompt_kernel, tq=tq, tk=tk, nq=nq, group=group, scale=hd ** -0.5),
        grid=(batch, heads // group, nq),
        in_specs=[
            pl.BlockSpec(memory_space=pltpu.SMEM),
            pl.BlockSpec((tq, group * hd), lambda b, h, i: (b * nq + i, h)),
            pl.BlockSpec((seq, group * hd), lambda b, h, i: (b, h)),
            pl.BlockSpec((seq, group * hd), lambda b, h, i: (b, h)),
            pl.BlockSpec((2 * tk, 2 * tk), lambda b, h, i: (0, 0)),
        ],
        out_specs=pl.BlockSpec((tq, group * hd), lambda b, h, i: (b * nq + i, h)),
        out_shape=jax.ShapeDtypeStruct((batch * seq, heads * hd), BF16),
        scratch_shapes=[pltpu.VMEM((group, seq, 2 * hd), BF16), pltpu.VMEM((group, seq, hd), BF16)],
        compiler_params=_params("parallel", "parallel", "arbitrary"),
        name="sb_prompt",
    )(logit_bias, q, k, v, _suffix_matrix(tk))


def _decode_phases(in_refs, o_ref, scratch_refs, *, pages, heads, hd, page, past_len, scale):
    q_ref, kn_ref, vn_ref, bias_ref, suffix_ref = in_refs[:5]
    k_refs = in_refs[5:5 + pages]
    v_refs = in_refs[5 + pages:5 + 2 * pages]
    qbd_ref, carry_ref, acc_ref = scratch_refs
    width = heads * hd

    def own_columns():
        head_of_col = lax.broadcasted_iota(jnp.int32, (heads, width), 1) // hd
        return head_of_col == lax.broadcasted_iota(jnp.int32, (heads, width), 0)

    def init():
        own = own_columns()
        bias = bias_ref[...]
        q_bd = jnp.where(own, q_ref[0].astype(F32), 0.0)
        qbd_ref[...] = q_bd.astype(BF16)
        k_new = kn_ref[0]
        v_new = vn_ref[0]
        z_new = jnp.sum(q_bd.astype(F32) * k_new, axis=-1, keepdims=True) * scale + bias
        visible = jnp.full((heads, 1), past_len, jnp.int32) < past_len
        sp = _softplus(z_new)
        log_om = jnp.where(visible, -sp, 0.0)
        a_new = jnp.where(visible, jnp.exp(z_new - sp), 0.0)
        carry_ref[...] = jnp.broadcast_to(log_om, carry_ref.shape)
        acc_ref[...] = a_new * v_new

    def token_rows(refs):
        blocks = []
        for ref in reversed(refs):
            x = jnp.swapaxes(ref[0].reshape(page, heads, hd).astype(BF16), 0, 1)
            blocks.append(jnp.concatenate([x[hh] for hh in range(heads)], axis=1))
        return jnp.concatenate(blocks, axis=0)

    def body():
        keys = token_rows(k_refs)
        values = token_rows(v_refs)
        z = _dot_nt(qbd_ref[...], keys) * scale + bias_ref[...]
        a, carry = _sb_block(z, None, suffix_ref[...], carry_ref[...], page)
        carry_ref[...] = carry
        acc_ref[...] += _dot(a.astype(BF16), values)

    def finish():
        acc = jnp.where(own_columns(), acc_ref[...], 0.0)
        o_ref[0] = jnp.sum(acc, axis=0, keepdims=True).astype(o_ref.dtype)

    return init, body, finish


def _decode_call_parts(q, k_new, v_new, k_pool, v_pool, first_page, page_table, logit_bias,
                       heads, hd, steps):
    batch, width = q.shape
    page = k_pool.shape[1] // heads
    n_pages = page_table.shape[1]
    assert page == LANES and hd == LANES and n_pages % steps == 0
    pages = n_pages // steps

    def page_spec(idx):
        return pl.BlockSpec(
            (1, page * heads, hd),
            lambda b, p, pt, idx=idx: (first_page + pt[b, n_pages - 1 - (p * pages + idx)], 0, 0))

    vec = pl.BlockSpec((1, 1, width), lambda b, p, pt: (b, 0, 0))
    static = dict(pages=pages, heads=heads, hd=hd, page=page, past_len=n_pages * page,
                  scale=hd ** -0.5)
    in_specs = ([vec, vec, vec,
                 pl.BlockSpec((heads, 1), lambda b, p, pt: (0, 0)),
                 pl.BlockSpec((2 * page, 2 * page), lambda b, p, pt: (0, 0))]
                + [page_spec(i) for i in range(pages)] * 2)
    args = [q.reshape(batch, 1, width), k_new.reshape(batch, 1, width),
            v_new.reshape(batch, 1, width), logit_bias.reshape(heads, 1), _suffix_matrix(page),
            *([k_pool] * pages), *([v_pool] * pages)]
    scratch = [pltpu.VMEM((heads, width), BF16), pltpu.VMEM((heads, page), F32),
               pltpu.VMEM((heads, width), F32)]
    return static, in_specs, args, vec, jax.ShapeDtypeStruct((batch, 1, width), BF16), scratch


def _sb_decode_kernel(pt_ref, *refs, n_in, **static):
    p = pl.program_id(1)
    init, body, finish = _decode_phases(refs[:n_in], refs[n_in], refs[n_in + 1:], **static)
    pl.when(p == 0)(init)
    body()
    pl.when(p == pl.num_programs(1) - 1)(finish)


def sb_decode_attention(q, k_new, v_new, k_pool, v_pool, first_page, page_table, logit_bias,
                        heads, hd):
    n_pages = page_table.shape[1]
    pages = DEC_PAGES_PER_STEP if n_pages % DEC_PAGES_PER_STEP == 0 else 1
    static, in_specs, args, out_spec, out_shape, scratch = _decode_call_parts(
        q, k_new, v_new, k_pool, v_pool, first_page, page_table, logit_bias, heads, hd,
        n_pages // pages)
    out = pl.pallas_call(
        functools.partial(_sb_decode_kernel, n_in=len(in_specs), **static),
        grid_spec=pltpu.PrefetchScalarGridSpec(
            num_scalar_prefetch=1, grid=(q.shape[0], n_pages // pages),
            in_specs=in_specs, out_specs=out_spec, scratch_shapes=scratch),
        out_shape=out_shape,
        compiler_params=_params("parallel", "arbitrary"),
        name="sb_decode",
    )(page_table, *args)
    return out.reshape(q.shape)


def _mlp_phases(y_ref, g_ref, wu_ref, wd_ref, gf_ref, out_ref, h_ref):
    def init():
        y = y_ref[...]
        h_ref[...] = _rmsnorm_rows(y, g_ref[...]).astype(BF16)
        out_ref[...] = y

    def body():
        u = jnp.maximum(_dot(h_ref[...], wu_ref[...]), 0.0)
        out_ref[...] += _dot((u * u).astype(BF16), wd_ref[...])

    def finish():
        if gf_ref is not None:
            out_ref[...] = _rmsnorm_rows(out_ref[...], gf_ref[...])

    return init, body, finish


def _mlp_kernel(*refs, final_norm, n_dec_in, dec_static):
    if n_dec_in:
        refs = refs[1:]
    n_mlp_in = 5 if final_norm else 4
    mlp_in = list(refs[:n_mlp_in]) + ([] if final_norm else [None])
    dec_in = refs[n_mlp_in:n_mlp_in + n_dec_in]
    pos = n_mlp_in + n_dec_in
    out_ref = refs[pos]
    pos += 1
    phases = []
    if n_dec_in:
        phases.append(_decode_phases(dec_in, refs[pos], refs[pos + 2:], **dec_static))
        pos += 1
    phases.insert(0, _mlp_phases(*mlp_in, out_ref, refs[pos]))

    def run_all(which):
        def run():
            for phase in phases:
                phase[which]()
        return run

    f = pl.program_id(1)
    pl.when(f == 0)(run_all(0))
    run_all(1)()
    pl.when(f == pl.num_programs(1) - 1)(run_all(2))


def mlp_residual(y, g, w_up, w_down, final_g=None, decode=None, *, tm=MLP_TM, tf=MLP_TF):
    m, d = y.shape
    d_ff = w_up.shape[1]
    tm = _row_tile(m, tm)
    assert d_ff % tf == 0
    grid = (m // tm, d_ff // tf)
    in_specs = [
        pl.BlockSpec((tm, d), lambda i, f, *_: (i, 0),
                     **({"pipeline_mode": pl.Buffered(1)} if decode else {})),
        pl.BlockSpec((1, d), lambda i, f, *_: (0, 0)),
        pl.BlockSpec((d, tf), lambda i, f, *_: (0, f)),
        pl.BlockSpec((tf, d), lambda i, f, *_: (f, 0)),
    ]
    args = [y, g.reshape(1, d), w_up, w_down]
    if final_g is not None:
        in_specs.append(pl.BlockSpec((1, d), lambda i, f, *_: (0, 0)))
        args.append(final_g.reshape(1, d))
    out_specs = [pl.BlockSpec((tm, d), lambda i, f, *_: (i, 0))]
    out_shape = [jax.ShapeDtypeStruct((m, d), F32)]
    scratch = [pltpu.VMEM((tm, d), BF16)]
    dec_static, n_dec_in, prefetch = None, 0, []
    if decode:
        assert decode["q"].shape[0] == grid[0]
        dec_static, dec_specs, dec_args, o_spec, o_shape, dec_scratch = _decode_call_parts(
            steps=grid[1], **decode)
        n_dec_in = len(dec_specs)
        in_specs += dec_specs
        args += dec_args
        out_specs.append(o_spec)
        out_shape.append(o_shape)
        scratch += dec_scratch
        prefetch = [decode["page_table"]]
    outs = pl.pallas_call(
        functools.partial(_mlp_kernel, final_norm=final_g is not None, n_dec_in=n_dec_in,
                          dec_static=dec_static),
        grid_spec=pltpu.PrefetchScalarGridSpec(
            num_scalar_prefetch=len(prefetch), grid=grid, in_specs=in_specs,
            out_specs=out_specs, scratch_shapes=scratch),
        out_shape=out_shape,
        compiler_params=_params("parallel", "arbitrary"),
        name="mlp_residual",
    )(*prefetch, *args)
    if decode:
        return outs[0], outs[1].reshape(decode["q"].shape)
    return outs[0]


def kernel(x_prompt, x_sample, state_C, state_n, state_m, cache_k, cache_v, page_table,
           norm_mix_g, norm_ffn_g, ml_w_in, ml_b_gate, ml_head_g, ml_w_out,
           sb_w_in, sb_logit_bias, sb_w_out, ffn_w_up, ffn_w_down, final_g):
    batch, seq, d = x_prompt.shape
    dec_batch, dec_seq, _ = x_sample.shape
    assert dec_seq == 1
    assert norm_mix_g.shape[0] == 2
    ml_heads, ml_dk, ml_dv = state_C.shape[2:]
    sb_heads, sb_hd = cache_k.shape[3:]
    hk, hv = ml_heads * ml_dk, ml_heads * ml_dv
    ml_main = 2 * hk + 2 * hv
    m = batch * seq

    ml_in = ml_w_in[0].astype(BF16)
    ml_gate = jnp.pad(ml_in[:, ml_main:], ((0, 0), (0, LANES - 2 * ml_heads)))
    ml_out = ml_w_out[0].astype(BF16)
    sb_in = sb_w_in[0].astype(BF16)
    sb_out = sb_w_out[0].astype(BF16)
    w_up = ffn_w_up.astype(BF16)
    w_down = ffn_w_down.astype(BF16)
    ml_segs = [(ml_main, BF16)]
    sb_segs = [(d, BF16), (d, F32), (d, F32)]

    ys = x_sample.reshape(dec_batch, d)
    qkvo, gates = norm_matmul(ys, norm_mix_g[0], ml_in, ml_segs, w_gate=ml_gate)
    hs, c_s, n_s, m_s = mlstm_sample(qkvo, gates, ml_b_gate[0], ml_head_g[0],
                                     state_C[0], state_n[0], state_m[0])
    ys = proj_residual(hs, ml_out, ys)
    ys = mlp_residual(ys, norm_ffn_g[0], w_up[0], w_down[0])
    q_s, k_s, v_s = norm_matmul(ys, norm_mix_g[1], sb_in, sb_segs)

    n_layers, n_pool, page = cache_k.shape[:3]
    pool_shape = (n_layers * n_pool, page * sb_heads, sb_hd)
    pool = dict(k_pool=cache_k.reshape(pool_shape), v_pool=cache_v.reshape(pool_shape),
                first_page=0, logit_bias=sb_logit_bias[0], heads=sb_heads, hd=sb_hd)
    n_hosts = 2
    per_host = dec_batch // n_hosts
    ride = (dec_batch % n_hosts == 0 and per_host == m // _row_tile(m, MLP_TM)
            and page_table.shape[1] % (w_up.shape[2] // MLP_TF) == 0)

    def decode_share(r):
        rows = slice(r * per_host, (r + 1) * per_host)
        return dict(q=q_s[rows], k_new=k_s[rows], v_new=v_s[rows], page_table=page_table[rows],
                    **pool)

    yp = x_prompt.reshape(m, d)
    qkvo, gates = norm_matmul(yp, norm_mix_g[0], ml_in, ml_segs, w_gate=ml_gate)
    hs, c_p, n_p, m_p = mlstm_prompt(qkvo, gates, ml_b_gate[0], ml_head_g[0],
                                     batch, seq, ml_heads, ml_dk, ml_dv)
    yp = proj_residual(hs, ml_out, yp)
    if ride:
        yp, o_first = mlp_residual(yp, norm_ffn_g[0], w_up[0], w_down[0], decode=decode_share(0))
    else:
        yp = mlp_residual(yp, norm_ffn_g[0], w_up[0], w_down[0])

    q_p, k_p, v_p = norm_matmul(yp, norm_mix_g[1], sb_in, sb_segs)
    o = sb_prompt_attention(q_p, k_p, v_p, sb_logit_bias[0], batch, seq, sb_heads, sb_hd)
    yp = proj_residual(o, sb_out, yp)
    if ride:
        yp, o_second = mlp_residual(yp, norm_ffn_g[1], w_up[1], w_down[1], final_g,
                                    decode=decode_share(1))
        o_s = jnp.concatenate([o_first, o_second], axis=0)
    else:
        yp = mlp_residual(yp, norm_ffn_g[1], w_up[1], w_down[1], final_g)
        o_s = sb_decode_attention(q_s, k_s, v_s, page_table=page_table, **pool)

    ys = proj_residual(o_s, sb_out, ys)
    ys = mlp_residual(ys, norm_ffn_g[1], w_up[1], w_down[1], final_g)

    kv_p = (1, batch, seq, sb_heads, sb_hd)
    kv_s = (1, dec_batch, dec_seq, sb_heads, sb_hd)
    return (yp.reshape(batch, seq, d), ys.reshape(dec_batch, dec_seq, d),
            c_p[None], n_p[None], m_p[None], k_p.reshape(kv_p), v_p.reshape(kv_p),
            c_s[None], n_s[None], m_s[None], k_s.reshape(kv_s), v_s.reshape(kv_s))
```

```python
import functools

import jax
import jax.numpy as jnp
from jax import lax
from jax.experimental import pallas as pl
from jax.experimental.pallas import tpu as pltpu

F32 = jnp.float32
BF16 = jnp.bfloat16

EPS = 1e-6
GATE_CAP = 15.0
LANES = 128
VMEM_LIMIT_BYTES = 56 * 1024 * 1024

ML_CHUNK = 128
ML_SEQS_PER_STEP = 2
SB_TQ = 512
SB_TK = 128
SB_HEAD_GROUP = 2
DEC_PAGES_PER_STEP = 4
MLP_TM = 1024
MLP_TF = 512


def _params(*sem):
    return pltpu.CompilerParams(dimension_semantics=sem, vmem_limit_bytes=VMEM_LIMIT_BYTES)


def _row_tile(m, want):
    return want if m % want == 0 else m


def _rmsnorm_rows(x, g):
    return x * lax.rsqrt(jnp.mean(x * x, axis=-1, keepdims=True) + EPS) * g


def _log_sigmoid(x):
    return jnp.minimum(x, 0.0) - jnp.log1p(jnp.exp(-jnp.abs(x)))


def _softplus(x):
    return jnp.maximum(x, 0.0) + jnp.log1p(jnp.exp(-jnp.abs(x)))


def _split_bf16(x):
    hi = x.astype(BF16)
    lo = (x - hi.astype(F32)).astype(BF16)
    return hi, lo


def _dot(a, b):
    return jnp.dot(a, b, preferred_element_type=F32)


def _dot_nt(a, b):
    return lax.dot_general(a, b, (((1,), (1,)), ((), ())), preferred_element_type=F32)


def _dot_tn(a, b):
    return lax.dot_general(a, b, (((0,), (0,)), ((), ())), preferred_element_type=F32)


def _norm_matmul_kernel(*refs, seg_tiles, has_gate):
    x_ref, g_ref, w_ref = refs[:3]
    pos = 3
    wg_ref = None
    if has_gate:
        wg_ref = refs[pos]
        pos += 1
    n_seg = len(seg_tiles)
    out_refs = refs[pos:pos + n_seg]
    pos += n_seg
    gate_ref = None
    if has_gate:
        gate_ref = refs[pos]
        pos += 1
    h_ref = refs[pos]

    j = pl.program_id(1)

    @pl.when(j == 0)
    def _():
        h_ref[...] = _rmsnorm_rows(x_ref[...], g_ref[...]).astype(BF16)
        if has_gate:
            gate_ref[...] = _dot(h_ref[...], wg_ref[...])

    acc = _dot(h_ref[...], w_ref[...])
    lo = 0
    for out_ref, nt in zip(out_refs, seg_tiles):
        @pl.when((j >= lo) & (j < lo + nt))
        def _(out_ref=out_ref):
            out_ref[...] = acc.astype(out_ref.dtype)
        lo += nt


def norm_matmul(x, g, w, segs, *, w_gate=None, tm=1024, tn=1024):
    m, k = x.shape
    tm = _row_tile(m, tm)
    seg_tiles = tuple(n // tn for n, _ in segs)
    assert all(n % tn == 0 for n, _ in segs) and m % tm == 0
    starts = []
    lo = 0
    for nt in seg_tiles:
        starts.append(lo)
        lo += nt
    n_tiles = lo

    in_specs = [
        pl.BlockSpec((tm, k), lambda i, j: (i, 0)),
        pl.BlockSpec((1, k), lambda i, j: (0, 0)),
        pl.BlockSpec((k, tn), lambda i, j: (0, j)),
    ]
    args = [x, g.reshape(1, k), w]
    if w_gate is not None:
        in_specs.append(pl.BlockSpec((k, LANES), lambda i, j: (0, 0)))
        args.append(w_gate)

    out_shape, out_specs = [], []
    for (n, dt), nt, st in zip(segs, seg_tiles, starts):
        out_shape.append(jax.ShapeDtypeStruct((m, n), dt))
        out_specs.append(pl.BlockSpec(
            (tm, tn), lambda i, j, st=st, nt=nt: (i, jnp.clip(j - st, 0, nt - 1))))
    if w_gate is not None:
        out_shape.append(jax.ShapeDtypeStruct((m, LANES), F32))
        out_specs.append(pl.BlockSpec((tm, LANES), lambda i, j: (i, 0)))

    return pl.pallas_call(
        functools.partial(_norm_matmul_kernel, seg_tiles=seg_tiles, has_gate=w_gate is not None),
        grid=(m // tm, n_tiles),
        in_specs=in_specs,
        out_specs=out_specs,
        out_shape=out_shape,
        scratch_shapes=[pltpu.VMEM((tm, k), BF16)],
        compiler_params=_params("parallel", "arbitrary"),
        name="norm_matmul",
    )(*args)


def _proj_residual_kernel(a_ref, w_ref, y_ref, out_ref):
    out_ref[...] = y_ref[...] + _dot(a_ref[...], w_ref[...])


def proj_residual(a, w, y, *, tm=512):
    m, k = a.shape
    n = w.shape[1]
    tm = _row_tile(m, tm)
    return pl.pallas_call(
        _proj_residual_kernel,
        grid=(m // tm,),
        in_specs=[
            pl.BlockSpec((tm, k), lambda i: (i, 0)),
            pl.BlockSpec((k, n), lambda i: (0, 0)),
            pl.BlockSpec((tm, n), lambda i: (i, 0)),
        ],
        out_specs=pl.BlockSpec((tm, n), lambda i: (i, 0)),
        out_shape=jax.ShapeDtypeStruct((m, n), F32),
        compiler_params=_params("parallel"),
        name="proj_residual",
    )(a, w, y)


def _gate_act(pre):
    capped = GATE_CAP * jnp.tanh(pre / GATE_CAP)
    return capped, _log_sigmoid(capped)


def _head_out(hcur, gh, o):
    hn = hcur * lax.rsqrt(jnp.mean(hcur * hcur, axis=-1, keepdims=True) + EPS)
    return hn * gh * jax.nn.sigmoid(o)


def _mlstm_chunk_kernel(q_ref, k_ref, v_ref, o_ref, gc_ref, gr_ref, bc_ref, br_ref, gh_ref,
                        tri_ref, tril_ref, hs_ref, c_ref, n_ref, m_ref, *, seqs, **static):
    c_idx = pl.program_id(1)

    @pl.when(c_idx == 0)
    def _():
        c_ref[...] = jnp.zeros_like(c_ref)
        n_ref[...] = jnp.zeros_like(n_ref)
        m_ref[...] = jnp.zeros_like(m_ref)

    for s in range(seqs):
        one = pl.ds(s, 1)
        _mlstm_chunk_sequence(q_ref.at[s], k_ref.at[s], v_ref.at[s], o_ref.at[s], gc_ref.at[s],
                              gr_ref.at[s], bc_ref, br_ref, gh_ref, tri_ref, tril_ref,
                              hs_ref.at[s], c_ref.at[one], n_ref.at[one], m_ref.at[one], **static)


def _mlstm_chunk_sequence(q_ref, k_ref, v_ref, o_ref, gc_ref, gr_ref, bc_ref, br_ref, gh_ref,
                          tri_ref, tril_ref, hs_ref, c_ref, n_ref, m_ref, *, heads, dk, dv, chunk):
    k_scale = dk ** -0.5
    tri = tri_ref[...]
    tril = tril_ref[...]
    li_c, lf_c = _gate_act(gc_ref[...] + br_ref[...])
    li_r, lf_r = _gate_act(gr_ref[...] + bc_ref[...])
    hi, lo = _split_bf16(lf_c)
    b_c = _dot(tril, hi) + _dot(tril, lo)
    hi, lo = _split_bf16(lf_r)
    b_r = _dot(hi, tri) + _dot(lo, tri)

    row = lax.broadcasted_iota(jnp.int32, (chunk, chunk), 0)
    col = lax.broadcasted_iota(jnp.int32, (chunk, chunk), 1)
    causal = col <= row

    for h in range(heads):
        q = q_ref[:, h * dk:(h + 1) * dk]
        k = k_ref[:, h * dk:(h + 1) * dk] * k_scale
        v = v_ref[:, h * dv:(h + 1) * dv]
        i_col = li_c[:, h:h + 1]
        b_col = b_c[:, heads + h:heads + h + 1]
        i_row = li_r[h:h + 1, :]
        b_row = b_r[heads + h:heads + h + 1, :]
        m_st = m_ref[0, h]
        c_st = c_ref[0, h]
        n_st = n_ref[0, h]

        log_d = jnp.where(causal, (b_col - b_row) + i_row, -jnp.inf)
        log_inter = b_col + m_st
        m_row = jnp.maximum(log_inter, jnp.max(log_d, axis=-1, keepdims=True))
        s = _dot_nt(q, k) * jnp.exp(log_d - m_row)
        w_inter = jnp.exp(log_inter - m_row)
        qf = q.astype(F32)
        num = w_inter * _dot(q, c_st.astype(BF16)) + _dot(s.astype(BF16), v)
        den = (w_inter * jnp.sum(qf * n_st, axis=-1, keepdims=True)
               + jnp.sum(s, axis=-1, keepdims=True))
        hcur = num / jnp.maximum(jnp.abs(den), jnp.exp(-m_row))
        gh = gh_ref[:, h * dv:(h + 1) * dv]
        o = o_ref[:, h * dv:(h + 1) * dv].astype(F32)
        hs_ref[:, h * dv:(h + 1) * dv] = _head_out(hcur, gh, o).astype(hs_ref.dtype)

        b_last = b_col[chunk - 1:chunk, :]
        log_w = (b_last - b_col) + i_col
        m_new = jnp.maximum(b_last + m_st, jnp.max(log_w, axis=0, keepdims=True))
        w = jnp.exp(log_w - m_new)
        decay = jnp.exp(b_last + m_st - m_new)
        wv = (w * v.astype(F32)).astype(BF16)
        c_ref[0, h] = decay * c_st + _dot_tn(k, wv)
        n_ref[0, h] = decay * n_st + jnp.sum(w * k.astype(F32), axis=0, keepdims=True)
        m_ref[0, h] = m_new


def mlstm_prompt(qkvo, gates, b_gate, g_head, batch, seq, heads, dk, dv):
    chunk = ML_CHUNK if seq % ML_CHUNK == 0 else seq
    n_chunks = seq // chunk
    hk, hv = heads * dk, heads * dv
    assert hk % LANES == 0 and hv % hk == 0
    kv_ratio = hv // hk
    seqs = ML_SEQS_PER_STEP if batch % ML_SEQS_PER_STEP == 0 else 1
    qkvo = qkvo.reshape(batch, seq, qkvo.shape[1])
    gates = gates.reshape(batch, seq, LANES)
    gates_r = jnp.swapaxes(gates[:, :, :2 * heads], 1, 2)
    tri = (jnp.arange(chunk)[:, None] <= jnp.arange(chunk)[None, :]).astype(BF16)

    hs, c_f, n_f, m_f = pl.pallas_call(
        functools.partial(_mlstm_chunk_kernel, seqs=seqs, heads=heads, dk=dk, dv=dv, chunk=chunk),
        grid=(batch // seqs, n_chunks),
        in_specs=[
            pl.BlockSpec((seqs, chunk, hk), lambda b, c: (b, c, 0)),
            pl.BlockSpec((seqs, chunk, hk), lambda b, c: (b, c, 1)),
            pl.BlockSpec((seqs, chunk, hv), lambda b, c: (b, c, 2 // kv_ratio)),
            pl.BlockSpec((seqs, chunk, hv), lambda b, c: (b, c, 2 // kv_ratio + 1)),
            pl.BlockSpec((seqs, chunk, LANES), lambda b, c: (b, c, 0)),
            pl.BlockSpec((seqs, 2 * heads, chunk), lambda b, c: (b, 0, c)),
            pl.BlockSpec((2 * heads, 1), lambda b, c: (0, 0)),
            pl.BlockSpec((1, LANES), lambda b, c: (0, 0)),
            pl.BlockSpec((1, hv), lambda b, c: (0, 0)),
            pl.BlockSpec((chunk, chunk), lambda b, c: (0, 0)),
            pl.BlockSpec((chunk, chunk), lambda b, c: (0, 0)),
        ],
        out_specs=[
            pl.BlockSpec((seqs, chunk, hv), lambda b, c: (b, c, 0)),
            pl.BlockSpec((seqs, heads, dk, dv), lambda b, c: (b, 0, 0, 0)),
            pl.BlockSpec((seqs, heads, 1, dk), lambda b, c: (b, 0, 0, 0)),
            pl.BlockSpec((seqs, heads, 1, 1), lambda b, c: (b, 0, 0, 0)),
        ],
        out_shape=[
            jax.ShapeDtypeStruct((batch, seq, hv), BF16),
            jax.ShapeDtypeStruct((batch, heads, dk, dv), F32),
            jax.ShapeDtypeStruct((batch, heads, 1, dk), F32),
            jax.ShapeDtypeStruct((batch, heads, 1, 1), F32),
        ],
        compiler_params=_params("parallel", "arbitrary"),
        name="mlstm_chunk",
    )(qkvo, qkvo, qkvo, qkvo, gates, gates_r,
      b_gate.reshape(2 * heads, 1),
      jnp.pad(b_gate, (0, LANES - 2 * heads)).reshape(1, LANES),
      g_head.reshape(1, hv), tri, tri.T)
    return (hs.reshape(batch * seq, hv), c_f, n_f.reshape(batch, heads, dk),
            m_f.reshape(batch, heads))


def _mlstm_step_kernel(qkvo_ref, gate_ref, br_ref, gh_ref, c_ref, n_ref, m_ref,
                       hs_ref, c_out, n_out, m_out, *, heads, dk, dv):
    hk, hv = heads * dk, heads * dv
    k_scale = dk ** -0.5
    li, lf = _gate_act(gate_ref[0] + br_ref[...])
    eye = (lax.broadcasted_iota(jnp.int32, (dk, dk), 0)
           == lax.broadcasted_iota(jnp.int32, (dk, dk), 1))

    def to_col(x_row):
        return jnp.sum(jnp.where(eye, x_row, 0.0), axis=-1, keepdims=True)

    for h in range(heads):
        q = qkvo_ref[0, :, h * dk:(h + 1) * dk].astype(F32)
        k = qkvo_ref[0, :, hk + h * dk:hk + (h + 1) * dk].astype(F32) * k_scale
        v = qkvo_ref[0, :, 2 * hk + h * dv:2 * hk + (h + 1) * dv].astype(F32)
        o = qkvo_ref[0, :, 2 * hk + hv + h * dv:2 * hk + hv + (h + 1) * dv].astype(F32)
        log_i = li[:, h:h + 1]
        log_f = lf[:, heads + h:heads + h + 1]
        c_st = c_ref[0, h]
        n_st = n_ref[0, h]
        m_st = m_ref[0, h]

        log_inter = log_f + m_st
        m_row = jnp.maximum(log_inter, log_i)
        s = jnp.sum(q * k, axis=-1, keepdims=True) * jnp.exp(log_i - m_row)
        w_inter = jnp.exp(log_inter - m_row)
        q_c = jnp.sum(to_col(q) * c_st, axis=0, keepdims=True)
        num = w_inter * q_c + s * v
        den = w_inter * jnp.sum(q * n_st, axis=-1, keepdims=True) + s
        hcur = num / jnp.maximum(jnp.abs(den), jnp.exp(-m_row))
        gh = gh_ref[:, h * dv:(h + 1) * dv]
        hs_ref[0, :, h * dv:(h + 1) * dv] = _head_out(hcur, gh, o).astype(hs_ref.dtype)

        m_new = jnp.maximum(log_f + m_st, log_i)
        w = jnp.exp(log_i - m_new)
        decay = jnp.exp(log_f + m_st - m_new)
        c_out[0, h] = decay * c_st + (w * to_col(k)) * v
        n_out[0, h] = decay * n_st + w * k
        m_out[0, h] = m_new


def mlstm_sample(qkvo, gates, b_gate, g_head, c0, n0, m0):
    batch, heads, dk, dv = c0.shape
    hk, hv = heads * dk, heads * dv
    width = qkvo.shape[1]
    hs, c_f, n_f, m_f = pl.pallas_call(
        functools.partial(_mlstm_step_kernel, heads=heads, dk=dk, dv=dv),
        grid=(batch,),
        in_specs=[
            pl.BlockSpec((1, 1, width), lambda b: (b, 0, 0)),
            pl.BlockSpec((1, 1, LANES), lambda b: (b, 0, 0)),
            pl.BlockSpec((1, LANES), lambda b: (0, 0)),
            pl.BlockSpec((1, hv), lambda b: (0, 0)),
            pl.BlockSpec((1, heads, dk, dv), lambda b: (b, 0, 0, 0)),
            pl.BlockSpec((1, heads, 1, dk), lambda b: (b, 0, 0, 0)),
            pl.BlockSpec((1, heads, 1, 1), lambda b: (b, 0, 0, 0)),
        ],
        out_specs=[
            pl.BlockSpec((1, 1, hv), lambda b: (b, 0, 0)),
            pl.BlockSpec((1, heads, dk, dv), lambda b: (b, 0, 0, 0)),
            pl.BlockSpec((1, heads, 1, dk), lambda b: (b, 0, 0, 0)),
            pl.BlockSpec((1, heads, 1, 1), lambda b: (b, 0, 0, 0)),
        ],
        out_shape=[
            jax.ShapeDtypeStruct((batch, 1, hv), BF16),
            jax.ShapeDtypeStruct((batch, heads, dk, dv), F32),
            jax.ShapeDtypeStruct((batch, heads, 1, dk), F32),
            jax.ShapeDtypeStruct((batch, heads, 1, 1), F32),
        ],
        compiler_params=_params("parallel"),
        name="mlstm_step",
    )(qkvo.reshape(batch, 1, width), gates.reshape(batch, 1, LANES),
      jnp.pad(b_gate, (0, LANES - 2 * heads)).reshape(1, LANES), g_head.reshape(1, hv),
      c0, n0.reshape(batch, heads, 1, dk), m0.reshape(batch, heads, 1, 1))
    return hs.reshape(batch, hv), c_f, n_f.reshape(batch, heads, dk), m_f.reshape(batch, heads)


def _suffix_matrix(tk):
    j = jnp.arange(tk)[:, None]
    s = jnp.arange(tk)[None, :]
    half = jnp.concatenate([(j >= s).astype(BF16), jnp.ones((tk, tk), BF16)], axis=1)
    return jnp.concatenate([half, half], axis=0)


def _sb_scan(z, mask, suffix, tk):
    n = z.shape[1] // tk
    neg = -z
    log_om = jnp.minimum(neg, 0.0) - jnp.log(1.0 + jnp.exp(jnp.minimum(z, neg)))
    sums = []
    for c in range(n):
        part = log_om[:, c * tk:(c + 1) * tk]
        if mask is not None and c == n - 1:
            part = jnp.where(mask, part, 0.0)
        hi, lo = _split_bf16(part)
        r = _dot(jnp.concatenate([hi, lo], axis=1), suffix)
        sums.append((r[:, :tk], r[:, tk:]))
    return sums


def _sb_weights(z, sums, mask, carry, tk):
    n = len(sums)
    args = [None] * n
    for c in range(n - 1, -1, -1):
        within, total = sums[c]
        args[c] = (z[:, c * tk:(c + 1) * tk] + carry) + within
        if mask is not None and c == n - 1:
            args[c] = jnp.where(mask, args[c], -jnp.inf)
        carry = carry + total
    return jnp.exp(jnp.concatenate(args, axis=1)), carry


def _sb_block(z, mask, suffix, carry, tk):
    return _sb_weights(z, _sb_scan(z, mask, suffix, tk), mask, carry, tk)


def _sb_prompt_kernel(bias_ref, q_ref, k_ref, v_ref, suffix_ref, o_ref, kb_ref, vb_ref,
                      *, tq, tk, nq, group, scale):
    hg = pl.program_id(1)
    i = pl.program_id(2)
    n_sub = tq // tk
    hd = q_ref.shape[1] // group

    def head_cols(g):
        return slice(g * hd, (g + 1) * hd)

    @pl.when(i == 0)
    def _():
        rows = kb_ref.shape[1]
        lane = lax.broadcasted_iota(jnp.int32, (rows, hd), 1)
        for g in range(group):
            kb_ref[g, :, :hd] = (k_ref[:, head_cols(g)] * scale).astype(BF16)
            b0 = jnp.full((rows, hd), bias_ref[hg * group + g], F32)
            p0 = b0.astype(BF16).astype(F32)
            b1 = b0 - p0
            p1 = b1.astype(BF16).astype(F32)
            p2 = b1 - p1
            pieces = jnp.where(lane == 0, p0, jnp.where(lane == 1, p1, jnp.where(lane == 2, p2, 0.0)))
            kb_ref[g, :, hd:] = pieces.astype(BF16)
            vb_ref[g] = v_ref[:, head_cols(g)].astype(BF16)

    suffix = suffix_ref[...]
    strict = (lax.broadcasted_iota(jnp.int32, (tk, tk), 1)
              < lax.broadcasted_iota(jnp.int32, (tk, tk), 0))
    ones = jnp.ones((tq, hd), BF16)
    q_aug = [jnp.concatenate([q_ref[:, head_cols(g)], ones], axis=1) for g in range(group)]

    def logits(start):
        return jnp.concatenate([_dot_nt(q_aug[g], kb_ref[g, start:start + tq, :])
                                for g in range(group)], axis=0)

    own = [(g * tq + r * tk, (r + 1) * tk) for g in range(group) for r in range(n_sub)]

    def scan(b, z):
        if b == 0:
            return [_sb_scan(z[row:row + tk, :width], strict, suffix, tk) for row, width in own]
        return _sb_scan(z, None, suffix, tk)

    def weigh(b, z, sums, carry):
        if b > 0:
            return _sb_weights(z, sums, None, carry, tk)
        weights, carries = [], []
        for (row, width), s in zip(own, sums):
            a, c_r = _sb_weights(z[row:row + tk, :width], s, strict, jnp.zeros((tk, tk), F32), tk)
            if width < tq:
                a = jnp.concatenate([a, jnp.zeros((tk, tq - width), F32)], axis=1)
            weights.append(a)
            carries.append(c_r)
        return jnp.concatenate(weights, axis=0), jnp.concatenate(carries, axis=0)

    for blk in range(nq):
        @pl.when(i == blk)
        def _(blk=blk):
            starts = [(blk - b) * tq for b in range(blk + 1)]
            n_blk = len(starts)
            zs = {b: logits(starts[b]) for b in range(min(2, n_blk))}
            sums = {0: scan(0, zs[0])}
            carry = None
            acc = [0.0] * group
            for b in range(n_blk):
                if b + 1 < n_blk:
                    sums[b + 1] = scan(b + 1, zs[b + 1])
                a, carry = weigh(b, zs.pop(b), sums.pop(b), carry)
                a = a.astype(BF16)
                for g in range(group):
                    acc[g] = acc[g] + _dot(a[g * tq:(g + 1) * tq],
                                           vb_ref[g, starts[b]:starts[b] + tq, :])
                if b + 2 < n_blk:
                    zs[b + 2] = logits(starts[b + 2])
            for g in range(group):
                o_ref[:, head_cols(g)] = acc[g].astype(o_ref.dtype)


def sb_prompt_attention(q, k, v, logit_bias, batch, seq, heads, hd):
    tk = SB_TK
    tq = SB_TQ if seq % SB_TQ == 0 else tk
    group = SB_HEAD_GROUP if heads % SB_HEAD_GROUP == 0 else 1
    assert tq % tk == 0 and seq % tq == 0 and hd == LANES
    nq = seq // tq
    return pl.pallas_call(
        functools.partial(_sb_prompt_kernel, tq=tq, tk=tk, nq=nq, group=group, scale=hd ** -0.5),
        grid=(batch, heads // group, nq),
        in_specs=[
            pl.BlockSpec(memory_space=pltpu.SMEM),
            pl.BlockSpec((tq, group * hd), lambda b, h, i: (b * nq + i, h)),
            pl.BlockSpec((seq, group * hd), lambda b, h, i: (b, h)),
            pl.BlockSpec((seq, group * hd), lambda b, h, i: (b, h)),
            pl.BlockSpec((2 * tk, 2 * tk), lambda b, h, i: (0, 0)),
        ],
        out_specs=pl.BlockSpec((tq, group * hd), lambda b, h, i: (b * nq + i, h)),
        out_shape=jax.ShapeDtypeStruct((batch * seq, heads * hd), BF16),
        scratch_shapes=[pltpu.VMEM((group, seq, 2 * hd), BF16), pltpu.VMEM((group, seq, hd), BF16)],
        compiler_params=_params("parallel", "parallel", "arbitrary"),
        name="sb_prompt",
    )(logit_bias, q, k, v, _suffix_matrix(tk))


def _decode_phases(in_refs, o_ref, scratch_refs, *, pages, heads, hd, page, past_len, scale):
    q_ref, kn_ref, vn_ref, bias_ref, suffix_ref = in_refs[:5]
    k_refs = in_refs[5:5 + pages]
    v_refs = in_refs[5 + pages:5 + 2 * pages]
    qbd_ref, carry_ref, acc_ref = scratch_refs
    width = heads * hd

    def own_columns():
        head_of_col = lax.broadcasted_iota(jnp.int32, (heads, width), 1) // hd
        return head_of_col == lax.broadcasted_iota(jnp.int32, (heads, width), 0)

    def init():
        own = own_columns()
        bias = bias_ref[...]
        q_bd = jnp.where(own, q_ref[0].astype(F32), 0.0)
        qbd_ref[...] = q_bd.astype(BF16)
        k_new = kn_ref[0]
        v_new = vn_ref[0]
        z_new = jnp.sum(q_bd.astype(F32) * k_new, axis=-1, keepdims=True) * scale + bias
        visible = jnp.full((heads, 1), past_len, jnp.int32) < past_len
        sp = _softplus(z_new)
        log_om = jnp.where(visible, -sp, 0.0)
        a_new = jnp.where(visible, jnp.exp(z_new - sp), 0.0)
        carry_ref[...] = jnp.broadcast_to(log_om, carry_ref.shape)
        acc_ref[...] = a_new * v_new

    def token_rows(refs):
        blocks = []
        for ref in reversed(refs):
            x = jnp.swapaxes(ref[0].reshape(page, heads, hd).astype(BF16), 0, 1)
            blocks.append(jnp.concatenate([x[hh] for hh in range(heads)], axis=1))
        return jnp.concatenate(blocks, axis=0)

    def body():
        keys = token_rows(k_refs)
        values = token_rows(v_refs)
        z = _dot_nt(qbd_ref[...], keys) * scale + bias_ref[...]
        a, carry = _sb_block(z, None, suffix_ref[...], carry_ref[...], page)
        carry_ref[...] = carry
        acc_ref[...] += _dot(a.astype(BF16), values)

    def finish():
        acc = jnp.where(own_columns(), acc_ref[...], 0.0)
        o_ref[0] = jnp.sum(acc, axis=0, keepdims=True).astype(o_ref.dtype)

    return init, body, finish


def _decode_call_parts(q, k_new, v_new, k_pool, v_pool, first_page, page_table, logit_bias,
                       heads, hd, steps):
    batch, width = q.shape
    page = k_pool.shape[1] // heads
    n_pages = page_table.shape[1]
    assert page == LANES and hd == LANES and n_pages % steps == 0
    pages = n_pages // steps

    def page_spec(idx):
        return pl.BlockSpec(
            (1, page * heads, hd),
            lambda b, p, pt, idx=idx: (first_page + pt[b, n_pages - 1 - (p * pages + idx)], 0, 0))

    vec = pl.BlockSpec((1, 1, width), lambda b, p, pt: (b, 0, 0))
    static = dict(pages=pages, heads=heads, hd=hd, page=page, past_len=n_pages * page,
                  scale=hd ** -0.5)
    in_specs = ([vec, vec, vec,
                 pl.BlockSpec((heads, 1), lambda b, p, pt: (0, 0)),
                 pl.BlockSpec((2 * page, 2 * page), lambda b, p, pt: (0, 0))]
                + [page_spec(i) for i in range(pages)] * 2)
    args = [q.reshape(batch, 1, width), k_new.reshape(batch, 1, width),
            v_new.reshape(batch, 1, width), logit_bias.reshape(heads, 1), _suffix_matrix(page),
            *([k_pool] * pages), *([v_pool] * pages)]
    scratch = [pltpu.VMEM((heads, width), BF16), pltpu.VMEM((heads, page), F32),
               pltpu.VMEM((heads, width), F32)]
    return static, in_specs, args, vec, jax.ShapeDtypeStruct((batch, 1, width), BF16), scratch


def _sb_decode_kernel(pt_ref, *refs, n_in, **static):
    p = pl.program_id(1)
    init, body, finish = _decode_phases(refs[:n_in], refs[n_in], refs[n_in + 1:], **static)
    pl.when(p == 0)(init)
    body()
    pl.when(p == pl.num_programs(1) - 1)(finish)


def sb_decode_attention(q, k_new, v_new, k_pool, v_pool, first_page, page_table, logit_bias,
                        heads, hd):
    n_pages = page_table.shape[1]
    pages = DEC_PAGES_PER_STEP if n_pages % DEC_PAGES_PER_STEP == 0 else 1
    static, in_specs, args, out_spec, out_shape, scratch = _decode_call_parts(
        q, k_new, v_new, k_pool, v_pool, first_page, page_table, logit_bias, heads, hd,
        n_pages // pages)
    out = pl.pallas_call(
        functools.partial(_sb_decode_kernel, n_in=len(in_specs), **static),
        grid_spec=pltpu.PrefetchScalarGridSpec(
            num_scalar_prefetch=1, grid=(q.shape[0], n_pages // pages),
            in_specs=in_specs, out_specs=out_spec, scratch_shapes=scratch),
        out_shape=out_shape,
        compiler_params=_params("parallel", "arbitrary"),
        name="sb_decode",
    )(page_table, *args)
    return out.reshape(q.shape)


def _mlp_phases(y_ref, g_ref, wu_ref, wd_ref, gf_ref, out_ref, h_ref):
    def init():
        y = y_ref[...]
        h_ref[...] = _rmsnorm_rows(y, g_ref[...]).astype(BF16)
        out_ref[...] = y

    def body():
        u = jnp.maximum(_dot(h_ref[...], wu_ref[...]), 0.0)
        out_ref[...] += _dot((u * u).astype(BF16), wd_ref[...])

    def finish():
        if gf_ref is not None:
            out_ref[...] = _rmsnorm_rows(out_ref[...], gf_ref[...])

    return init, body, finish


def _mlp_kernel(*refs, final_norm, n_dec_in, dec_static):
    if n_dec_in:
        refs = refs[1:]
    n_mlp_in = 5 if final_norm else 4
    mlp_in = list(refs[:n_mlp_in]) + ([] if final_norm else [None])
    dec_in = refs[n_mlp_in:n_mlp_in + n_dec_in]
    pos = n_mlp_in + n_dec_in
    out_ref = refs[pos]
    pos += 1
    phases = []
    if n_dec_in:
        phases.append(_decode_phases(dec_in, refs[pos], refs[pos + 2:], **dec_static))
        pos += 1
    phases.insert(0, _mlp_phases(*mlp_in, out_ref, refs[pos]))

    def run_all(which):
        def run():
            for phase in phases:
                phase[which]()
        return run

    f = pl.program_id(1)
    pl.when(f == 0)(run_all(0))
    run_all(1)()
    pl.when(f == pl.num_programs(1) - 1)(run_all(2))


def mlp_residual(y, g, w_up, w_down, final_g=None, decode=None, *, tm=MLP_TM, tf=MLP_TF):
    m, d = y.shape
    d_ff = w_up.shape[1]
    tm = _row_tile(m, tm)
    assert d_ff % tf == 0
    grid = (m // tm, d_ff // tf)
    in_specs = [
        pl.BlockSpec((tm, d), lambda i, f, *_: (i, 0),
                     **({"pipeline_mode": pl.Buffered(1)} if decode else {})),
        pl.BlockSpec((1, d), lambda i, f, *_: (0, 0)),
        pl.BlockSpec((d, tf), lambda i, f, *_: (0, f)),
        pl.BlockSpec((tf, d), lambda i, f, *_: (f, 0)),
    ]
    args = [y, g.reshape(1, d), w_up, w_down]
    if final_g is not None:
        in_specs.append(pl.BlockSpec((1, d), lambda i, f, *_: (0, 0)))
        args.append(final_g.reshape(1, d))
    out_specs = [pl.BlockSpec((tm, d), lambda i, f, *_: (i, 0))]
    out_shape = [jax.ShapeDtypeStruct((m, d), F32)]
    scratch = [pltpu.VMEM((tm, d), BF16)]
    dec_static, n_dec_in, prefetch = None, 0, []
    if decode:
        assert decode["q"].shape[0] == grid[0]
        dec_static, dec_specs, dec_args, o_spec, o_shape, dec_scratch = _decode_call_parts(
            steps=grid[1], **decode)
        n_dec_in = len(dec_specs)
        in_specs += dec_specs
        args += dec_args
        out_specs.append(o_spec)
        out_shape.append(o_shape)
        scratch += dec_scratch
        prefetch = [decode["page_table"]]
    outs = pl.pallas_call(
        functools.partial(_mlp_kernel, final_norm=final_g is not None, n_dec_in=n_dec_in,
                          dec_static=dec_static),
        grid_spec=pltpu.PrefetchScalarGridSpec(
            num_scalar_prefetch=len(prefetch), grid=grid, in_specs=in_specs,
            out_specs=out_specs, scratch_shapes=scratch),
        out_shape=out_shape,
        compiler_params=_params("parallel", "arbitrary"),
        name="mlp_residual",
    )(*prefetch, *args)
    if decode:
        return outs[0], outs[1].reshape(decode["q"].shape)
    return outs[0]


def kernel(x_prompt, x_sample, state_C, state_n, state_m, cache_k, cache_v, page_table,
           norm_mix_g, norm_ffn_g, ml_w_in, ml_b_gate, ml_head_g, ml_w_out,
           sb_w_in, sb_logit_bias, sb_w_out, ffn_w_up, ffn_w_down, final_g):
    batch, seq, d = x_prompt.shape
    dec_batch, dec_seq, _ = x_sample.shape
    assert dec_seq == 1
    assert norm_mix_g.shape[0] == 2
    ml_heads, ml_dk, ml_dv = state_C.shape[2:]
    sb_heads, sb_hd = cache_k.shape[3:]
    hk, hv = ml_heads * ml_dk, ml_heads * ml_dv
    ml_main = 2 * hk + 2 * hv
    m = batch * seq

    ml_in = ml_w_in[0].astype(BF16)
    ml_gate = jnp.pad(ml_in[:, ml_main:], ((0, 0), (0, LANES - 2 * ml_heads)))
    ml_out = ml_w_out[0].astype(BF16)
    sb_in = sb_w_in[0].astype(BF16)
    sb_out = sb_w_out[0].astype(BF16)
    w_up = ffn_w_up.astype(BF16)
    w_down = ffn_w_down.astype(BF16)
    ml_segs = [(ml_main, BF16)]
    sb_segs = [(d, BF16), (d, F32), (d, F32)]

    ys = x_sample.reshape(dec_batch, d)
    qkvo, gates = norm_matmul(ys, norm_mix_g[0], ml_in, ml_segs, w_gate=ml_gate)
    hs, c_s, n_s, m_s = mlstm_sample(qkvo, gates, ml_b_gate[0], ml_head_g[0],
                                     state_C[0], state_n[0], state_m[0])
    ys = proj_residual(hs, ml_out, ys)
    ys = mlp_residual(ys, norm_ffn_g[0], w_up[0], w_down[0])
    q_s, k_s, v_s = norm_matmul(ys, norm_mix_g[1], sb_in, sb_segs)

    n_layers, n_pool, page = cache_k.shape[:3]
    pool_shape = (n_layers * n_pool, page * sb_heads, sb_hd)
    pool = dict(k_pool=cache_k.reshape(pool_shape), v_pool=cache_v.reshape(pool_shape),
                first_page=0, logit_bias=sb_logit_bias[0], heads=sb_heads, hd=sb_hd)
    n_hosts = 2
    per_host = dec_batch // n_hosts
    ride = (dec_batch % n_hosts == 0 and per_host == m // _row_tile(m, MLP_TM)
            and page_table.shape[1] % (w_up.shape[2] // MLP_TF) == 0)

    def decode_share(r):
        rows = slice(r * per_host, (r + 1) * per_host)
        return dict(q=q_s[rows], k_new=k_s[rows], v_new=v_s[rows], page_table=page_table[rows],
                    **pool)

    yp = x_prompt.reshape(m, d)
    qkvo, gates = norm_matmul(yp, norm_mix_g[0], ml_in, ml_segs, w_gate=ml_gate)
    hs, c_p, n_p, m_p = mlstm_prompt(qkvo, gates, ml_b_gate[0], ml_head_g[0],
                                     batch, seq, ml_heads, ml_dk, ml_dv)
    yp = proj_residual(hs, ml_out, yp)
    if ride:
        yp, o_first = mlp_residual(yp, norm_ffn_g[0], w_up[0], w_down[0], decode=decode_share(0))
    else:
        yp = mlp_residual(yp, norm_ffn_g[0], w_up[0], w_down[0])

    q_p, k_p, v_p = norm_matmul(yp, norm_mix_g[1], sb_in, sb_segs)
    o = sb_prompt_attention(q_p, k_p, v_p, sb_logit_bias[0], batch, seq, sb_heads, sb_hd)
    yp = proj_residual(o, sb_out, yp)
    if ride:
        yp, o_second = mlp_residual(yp, norm_ffn_g[1], w_up[1], w_down[1], final_g,
                                    decode=decode_share(1))
        o_s = jnp.concatenate([o_first, o_second], axis=0)
    else:
        yp = mlp_residual(yp, norm_ffn_g[1], w_up[1], w_down[1], final_g)
        o_s = sb_decode_attention(q_s, k_s, v_s, page_table=page_table, **pool)

    ys = proj_residual(o_s, sb_out, ys)
    ys = mlp_residual(ys, norm_ffn_g[1], w_up[1], w_down[1], final_g)

    kv_p = (1, batch, seq, sb_heads, sb_hd)
    kv_s = (1, dec_batch, dec_seq, sb_heads, sb_hd)
    return (yp.reshape(batch, seq, d), ys.reshape(dec_batch, dec_seq, d),
            c_p[None], n_p[None], m_p[None], k_p.reshape(kv_p), v_p.reshape(kv_p),
            c_s[None], n_s[None], m_s[None], k_s.reshape(kv_s), v_s.reshape(kv_s))
```
